```python
import math
import jax, jax.numpy as jnp
from jax import lax
import numpy as np

D_MODEL = 1024
BATCH = 1
SEQ = 16384
DEPTH = 1
DEC_BATCH = 2
DEC_SEQ = 8192
PAST_LEN = 128

N_GROUPS = 3
WINDOWS = (128, 512, 2048)
DILATIONS = (1, 4, 16)
HEADS_PER_GROUP = 8
HEAD_DIM = 64
ATTN_WIDTH = HEADS_PER_GROUP * HEAD_DIM
QKV_WIDTH = N_GROUPS * ATTN_WIDTH
CONV_WIDTH = 512
CONV_K = 3
N_BRANCH = 2
IN_WIDTH = 3 * QKV_WIDTH + ATTN_WIDTH + 4 * CONV_WIDTH + N_BRANCH * D_MODEL
RMS_EPS = 1e-6
NEG_BIG = -1e30

kernel_name = "hybrid_dilated_attn_shortconv_encoder"


def _alibi_slopes():
    n = N_GROUPS * HEADS_PER_GROUP
    s = 2.0 ** (-8.0 * np.arange(1, n + 1) / n)
    return jnp.asarray(s.astype(np.float32).reshape(N_GROUPS, HEADS_PER_GROUP))


def _split_points():
    sizes = [QKV_WIDTH, QKV_WIDTH, QKV_WIDTH, ATTN_WIDTH,
             CONV_WIDTH, CONV_WIDTH, CONV_WIDTH, CONV_WIDTH, N_BRANCH * D_MODEL]
    return [int(v) for v in np.cumsum(sizes)[:-1]]


def _rmsnorm(x, g):
    xf = x.astype(jnp.float32)
    y = xf * lax.rsqrt(jnp.mean(xf * xf, axis=-1, keepdims=True) + RMS_EPS) * g.astype(jnp.float32)
    return y.astype(x.dtype)


def _dilated_band_attention(q, k, v, dil, radius, slopes):
    B, S, H, Dh = q.shape
    L = S // dil
    blk = radius
    nb = -(-L // blk)
    Lp = nb * blk

    def to_res(t):
        return t.reshape(B, L, dil, H, Dh).transpose(0, 2, 1, 3, 4)

    qr, kr, vr = to_res(q), to_res(k), to_res(v)
    qb = jnp.pad(qr, ((0, 0), (0, 0), (0, Lp - L), (0, 0), (0, 0))).reshape(B, dil, nb, blk, H, Dh)

    def key_windows(t):
        tp = jnp.pad(t, ((0, 0), (0, 0), (blk, blk + Lp - L), (0, 0), (0, 0)))
        tb = tp.reshape(B, dil, nb + 2, blk, H, Dh)
        return jnp.concatenate([tb[:, :, :-2], tb[:, :, 1:-1], tb[:, :, 2:]], axis=3)

    kw, vw = key_windows(kr), key_windows(vr)
    s = jnp.einsum('brnqhd,brnkhd->brhnqk', qb, kw,
                   preferred_element_type=jnp.float32) * (1.0 / math.sqrt(Dh))

    n_idx = jnp.arange(nb)[:, None, None]
    qpos = n_idx * blk + jnp.arange(blk)[None, :, None]
    kpos = (n_idx - 1) * blk + jnp.arange(3 * blk)[None, None, :]
    rel = jnp.abs(kpos - qpos)
    valid = (rel <= radius) & (kpos >= 0) & (kpos < L)
    dist = (rel * dil).astype(jnp.float32)
    bias = -slopes[:, None, None, None] * dist[None]
    s = jnp.where(valid, s + bias, NEG_BIG)

    m = jnp.max(s, axis=-1, keepdims=True)
    p = jnp.exp(s - m)
    den = jnp.sum(p, axis=-1)
    o = jnp.einsum('brhnqk,brnkhd->brnqhd', p, vw.astype(jnp.float32))
    o = o / jnp.transpose(den, (0, 1, 3, 4, 2))[..., None]
    lse = (m[..., 0] + jnp.log(den)).transpose(0, 1, 3, 4, 2)

    o = o.reshape(B, dil, Lp, H, Dh)[:, :, :L].transpose(0, 2, 1, 3, 4).reshape(B, S, H, Dh)
    lse = lse.reshape(B, dil, Lp, H)[:, :, :L].transpose(0, 2, 1, 3).reshape(B, S, H)
    return o, lse


def _layer(x, norm_g, w_in, b_gate, conv_w, w_attn_out, w_conv_out, w_o):
    B, S, _ = x.shape
    hn = _rmsnorm(x, norm_g)
    proj = jnp.einsum('bsd,de->bse', hn, w_in)
    q, k, v, z_a, hc, gb, gc, z_b, g = jnp.split(proj, _split_points(), axis=-1)

    shp = (B, S, N_GROUPS, HEADS_PER_GROUP, HEAD_DIM)
    q, k, v = q.reshape(shp), k.reshape(shp), v.reshape(shp)
    slopes = _alibi_slopes()
    outs, lses = [], []
    for gi in range(N_GROUPS):
        dil = DILATIONS[gi]
        radius = WINDOWS[gi] // (2 * dil)
        o, l = _dilated_band_attention(q[:, :, gi], k[:, :, gi], v[:, :, gi], dil, radius, slopes[gi])
        outs.append(o)
        lses.append(l)
    wts = jax.nn.softmax(jnp.stack(lses), axis=0)
    attn = jnp.einsum('gbsh,gbshd->bshd', wts, jnp.stack(outs))
    attn = attn.reshape(B, S, ATTN_WIDTH).astype(x.dtype)
    a = jnp.einsum('bsc,cd->bsd', jax.nn.silu(z_a) * attn, w_attn_out)

    u = gc * hc
    up = jnp.pad(u, ((0, 0), (1, 1), (0, 0)))
    conv = conv_w[0] * up[:, :-2] + conv_w[1] * up[:, 1:-1] + conv_w[2] * up[:, 2:]
    c = jnp.einsum('bsc,cd->bsd', jax.nn.silu(z_b) * (gb * conv), w_conv_out)

    gates = jax.nn.sigmoid(g + b_gate)
    g_a, g_c = jnp.split(gates, N_BRANCH, axis=-1)
    merged = g_a * a + g_c * c
    return x + jnp.einsum('bsd,de->bse', merged, w_o)


def setup_inputs(seed: int = 0) -> dict:
    key = jax.random.key(seed)
    ks = jax.random.split(key, 10)
    f32 = jnp.float32
    x_prompt = jax.random.normal(ks[0], (BATCH, SEQ, D_MODEL), f32)
    x_sample = jax.random.normal(ks[1], (DEC_BATCH, DEC_SEQ, D_MODEL), f32)
    norm_g = 1.0 + 0.02 * jax.random.normal(ks[2], (DEPTH, D_MODEL), f32)
    w_in = jax.random.normal(ks[3], (DEPTH, D_MODEL, IN_WIDTH), f32) * D_MODEL ** -0.5
    b_gate = 0.01 * jax.random.normal(ks[4], (DEPTH, N_BRANCH * D_MODEL), f32)
    conv_w = jax.random.normal(ks[5], (DEPTH, CONV_K, CONV_WIDTH), f32) * CONV_K ** -0.5
    w_attn_out = jax.random.normal(ks[6], (DEPTH, ATTN_WIDTH, D_MODEL), f32) * ATTN_WIDTH ** -0.5
    w_conv_out = jax.random.normal(ks[7], (DEPTH, CONV_WIDTH, D_MODEL), f32) * CONV_WIDTH ** -0.5
    w_o = jax.random.normal(ks[8], (DEPTH, D_MODEL, D_MODEL), f32) * D_MODEL ** -0.5
    final_g = 1.0 + 0.02 * jax.random.normal(ks[9], (D_MODEL,), f32)
    return {"x_prompt": x_prompt, "x_sample": x_sample, "norm_g": norm_g, "w_in": w_in,
            "b_gate": b_gate, "conv_w": conv_w, "w_attn_out": w_attn_out,
            "w_conv_out": w_conv_out, "w_o": w_o, "final_g": final_g}


def reference(x_prompt, x_sample, norm_g, w_in, b_gate, conv_w, w_attn_out, w_conv_out, w_o, final_g):
    hp, hs = x_prompt, x_sample
    for l in range(DEPTH):
        hp = _layer(hp, norm_g[l], w_in[l], b_gate[l], conv_w[l], w_attn_out[l], w_conv_out[l], w_o[l])
        hs = _layer(hs, norm_g[l], w_in[l], b_gate[l], conv_w[l], w_attn_out[l], w_conv_out[l], w_o[l])
    y_prompt = _rmsnorm(hp, final_g)
    y_sample = _rmsnorm(hs, final_g)
    return (y_prompt, y_sample)
```

```python
import functools
import math

import jax
import jax.numpy as jnp
import numpy as np
from jax import lax
from jax.experimental import pallas as pl
from jax.experimental.pallas import tpu as pltpu

F32 = jnp.float32
BF16 = jnp.bfloat16

D_MODEL = 1024
N_GROUPS = 3
DILATIONS = (1, 4, 16)
WINDOWS = (128, 512, 2048)
HEADS = 8
HEAD_DIM = 64
ATTN_W = HEADS * HEAD_DIM
QKV_W = N_GROUPS * ATTN_W
CONV_W = 512
IN_W = 3 * QKV_W + ATTN_W + 4 * CONV_W + 2 * D_MODEL
RADIUS = 64
RMS_EPS = 1e-6
NEG_BIG = -1e30

LANES = 128
N_SLAB = D_MODEL // LANES
N_PAIR = ATTN_W // LANES

_Q0, _K0, _V0 = 0, QKV_W, 2 * QKV_W
_ZA = 3 * QKV_W
_HC = _ZA + ATTN_W
_GB = _HC + CONV_W
_GC = _GB + CONV_W
_ZB = _GC + CONV_W
_G = _ZB + CONV_W

T_PROJ = 512
T_ATTN = 1024
T_OUT = 1024
VMEM_LIMIT = 56 * 1024 * 1024


def _alibi_slopes():
    n = N_GROUPS * HEADS
    s = 2.0 ** (-8.0 * np.arange(1, n + 1) / n)
    return s.astype(np.float32).reshape(N_GROUPS, HEADS)


def _sigmoid(z):
    return 1.0 / (1.0 + jnp.exp(-z))


def _proj_kernel(x_ref, g_ref, w_ref, bg_ref,
                 kvq0_ref, kvq1_ref, kvq2_ref, za_ref, u_ref, pre_ref, ga_ref, gc_ref,
                 hslab_ref):
    t = x_ref.shape[1]
    x = x_ref[0]
    ms = jnp.mean(x * x, axis=-1, keepdims=True)
    hn = x * lax.rsqrt(ms + RMS_EPS) * g_ref[...]
    hb = hn.astype(BF16)

    def mm(h, c0, width):
        return jnp.dot(h, w_ref[:, c0:c0 + width], preferred_element_type=F32)

    kvq0_ref[0, 0, :, 0:ATTN_W] = mm(hb, _K0, ATTN_W).astype(BF16)
    kvq0_ref[0, 0, :, ATTN_W:2 * ATTN_W] = mm(hb, _V0, ATTN_W).astype(BF16)
    kvq0_ref[0, 0, :, 2 * ATTN_W:] = (mm(hb, _Q0, ATTN_W) * 0.125).astype(BF16)

    for s in range(N_SLAB):
        hslab_ref[s] = hn[:, s * LANES:(s + 1) * LANES]
    for gi, out_ref in ((1, kvq1_ref), (2, kvq2_ref)):
        d = DILATIONS[gi]
        rows = t // d
        parts = []
        for r in range(d):
            parts.append(jnp.concatenate(
                [hslab_ref[s, pl.ds(r, rows, stride=d), :] for s in range(N_SLAB)], axis=-1))
        hr = jnp.concatenate(parts, axis=0).astype(BF16)
        for c, (col, scale) in enumerate(((_K0, None), (_V0, None), (_Q0, 0.125))):
            res = mm(hr, col + gi * ATTN_W, ATTN_W)
            if scale is not None:
                res = res * scale
            res = res.astype(BF16)
            for r in range(d):
                out_ref[0, r, :, c * ATTN_W:(c + 1) * ATTN_W] = res[r * rows:(r + 1) * rows]

    za = mm(hb, _ZA, ATTN_W)
    za_ref[0] = (za * _sigmoid(za)).astype(BF16)
    u_ref[0] = (mm(hb, _GC, CONV_W) * mm(hb, _HC, CONV_W)).astype(BF16)
    zb = mm(hb, _ZB, CONV_W)
    pre_ref[0] = (zb * _sigmoid(zb) * mm(hb, _GB, CONV_W)).astype(BF16)
    ga_ref[0] = _sigmoid(mm(hb, _G, D_MODEL) + bg_ref[:, 0:D_MODEL]).astype(BF16)
    gc_ref[0] = _sigmoid(mm(hb, _G + D_MODEL, D_MODEL) + bg_ref[:, D_MODEL:]).astype(BF16)


def _projection(x, norm_g, w_bf, b_gate):
    b, s, _ = x.shape
    t = T_PROJ
    grid = (b, s // t)
    tok = lambda width: pl.BlockSpec((1, t, width), lambda bi, i: (bi, i, 0))
    res = lambda d: pl.BlockSpec((1, d, t // d, QKV_W), lambda bi, i: (bi, 0, i, 0))
    const = lambda shape: pl.BlockSpec(shape, lambda bi, i: (0,) * len(shape))
    out_shape = (
        jax.ShapeDtypeStruct((b, 1, s, QKV_W), BF16),
        jax.ShapeDtypeStruct((b, 4, s // 4, QKV_W), BF16),
        jax.ShapeDtypeStruct((b, 16, s // 16, QKV_W), BF16),
        jax.ShapeDtypeStruct((b, s, ATTN_W), BF16),
        jax.ShapeDtypeStruct((b, s, CONV_W), BF16),
        jax.ShapeDtypeStruct((b, s, CONV_W), BF16),
        jax.ShapeDtypeStruct((b, s, D_MODEL), BF16),
        jax.ShapeDtypeStruct((b, s, D_MODEL), BF16),
    )
    return pl.pallas_call(
        _proj_kernel,
        grid=grid,
        in_specs=[
            tok(D_MODEL),
            const((1, D_MODEL)),
            pl.BlockSpec((D_MODEL, IN_W), lambda bi, i: (0, 0), pipeline_mode=pl.Buffered(1)),
            const((1, 2 * D_MODEL)),
        ],
        out_specs=(res(1), res(4), res(16), tok(ATTN_W), tok(CONV_W), tok(CONV_W),
                   tok(D_MODEL), tok(D_MODEL)),
        out_shape=out_shape,
        scratch_shapes=[pltpu.VMEM((N_SLAB, t, LANES), F32)],
        compiler_params=pltpu.CompilerParams(
            dimension_semantics=("arbitrary", "arbitrary"), vmem_limit_bytes=VMEM_LIMIT),
        name="proj",
    )(x, norm_g.reshape(1, D_MODEL), w_bf, b_gate.reshape(1, 2 * D_MODEL))


def _block_attention(q, kwin, vwin, bias_ref, col_ok):
    tq = q.shape[0]
    low = lax.broadcasted_iota(jnp.int32, (1, LANES), 1) < HEAD_DIM
    zero = jnp.zeros((), BF16)
    for hp in range(N_PAIR):
        sl = slice(hp * LANES, (hp + 1) * LANES)
        qp, kp, vp = q[:, sl], kwin[:, sl], vwin[:, sl]
        outs, lses = [], []
        for half in range(2):
            keep = low if half == 0 else jnp.logical_not(low)
            qh = jnp.where(keep, qp, zero)
            s = lax.dot_general(qh, kp, (((1,), (1,)), ((), ())), preferred_element_type=F32)
            s = s + bias_ref[2 * hp + half]
            if col_ok is not None:
                s = jnp.where(col_ok, s, NEG_BIG)
            m = jnp.max(s, axis=-1, keepdims=True)
            p = jnp.exp(s - m)
            den = jnp.sum(p, axis=-1, keepdims=True)
            o = jnp.dot(p.astype(BF16), vp, preferred_element_type=F32)
            outs.append(o * (1.0 / den))
            lses.append(jnp.broadcast_to(m + jnp.log(den), (tq, LANES)))
        yield hp, jnp.where(low, outs[0], outs[1]), jnp.where(low, lses[0], lses[1])


def _attn_kernel(c0_ref, p0_ref, n0_ref, c1_ref, p1_ref, n1_ref, c2_ref, p2_ref, n2_ref,
                 attn_ref,
                 b0_ref, b1_ref, b2_ref, o0_ref, l0_ref, o1_ref, l1_ref, o2_ref, l2_ref):
    bi, ti = pl.program_id(0), pl.program_id(1)
    first, last = ti == 0, ti == pl.num_programs(1) - 1
    tt = attn_ref.shape[1]
    slopes = _alibi_slopes()

    @pl.when(jnp.logical_and(bi == 0, ti == 0))
    def _init_bias():
        for gi, bref in enumerate((b0_ref, b1_ref, b2_ref)):
            tq, tk = bref.shape[1], bref.shape[2]
            qi = lax.broadcasted_iota(jnp.int32, (tq, tk), 0)
            kj = lax.broadcasted_iota(jnp.int32, (tq, tk), 1)
            rel = jnp.abs(kj - RADIUS - qi)
            dist = (rel * DILATIONS[gi]).astype(F32)
            for h in range(HEADS):
                bref[h] = jnp.where(rel <= RADIUS, -float(slopes[gi, h]) * dist, NEG_BIG)

    def edge_mask(tk, lo, hi):
        col = lax.broadcasted_iota(jnp.int32, (1, tk), 1)
        ok = jnp.ones((1, tk), jnp.bool_)
        if lo:
            ok = jnp.logical_and(ok, jnp.logical_or(col >= RADIUS, jnp.logical_not(first)))
        if hi:
            ok = jnp.logical_and(ok, jnp.logical_or(col < tk - RADIUS, jnp.logical_not(last)))
        return ok

    def run_block(q, kv, bias_ref, col_ok, o_ref, l_ref, row0, d):
        tq = q.shape[0]
        for hp, o, lse in _block_attention(q, kv[:, 0:ATTN_W], kv[:, ATTN_W:], bias_ref, col_ok):
            rows = pl.ds(row0, tq) if d == 1 else pl.ds(row0, tq, stride=d)
            o_ref[hp, rows, :] = o
            l_ref[hp, rows, :] = lse

    kv_cols = slice(0, 2 * ATTN_W)
    q_cols = slice(2 * ATTN_W, QKV_W)

    tq = b0_ref.shape[1]
    nblk = tt // tq
    kv = jnp.concatenate([p0_ref[0, 0], c0_ref[0, 0, 0:tq + RADIUS, kv_cols]], axis=0)
    run_block(c0_ref[0, 0, 0:tq, q_cols], kv, b0_ref, edge_mask(tq + 2 * RADIUS, True, False),
              o0_ref, l0_ref, 0, 1)

    def mid0(n, carry):
        q0 = pl.multiple_of(n * tq, tq)
        k0 = pl.multiple_of(n * tq - RADIUS, RADIUS)
        run_block(c0_ref[0, 0, pl.ds(q0, tq), q_cols],
                  c0_ref[0, 0, pl.ds(k0, tq + 2 * RADIUS), kv_cols], b0_ref, None,
                  o0_ref, l0_ref, q0, 1)
        return carry
    lax.fori_loop(1, nblk - 1, mid0, 0)

    kv = jnp.concatenate([c0_ref[0, 0, tt - tq - RADIUS:tt, kv_cols], n0_ref[0, 0]], axis=0)
    run_block(c0_ref[0, 0, tt - tq:tt, q_cols], kv, b0_ref, edge_mask(tq + 2 * RADIUS, False, True),
              o0_ref, l0_ref, tt - tq, 1)

    d = DILATIONS[1]
    rr = tt // d
    tq = b1_ref.shape[1]
    assert rr == 2 * tq

    def res1(r, carry):
        kv = jnp.concatenate([p1_ref[0, r], c1_ref[0, r, 0:tq + RADIUS, kv_cols]], axis=0)
        run_block(c1_ref[0, r, 0:tq, q_cols], kv, b1_ref, edge_mask(tq + 2 * RADIUS, True, False),
                  o1_ref, l1_ref, r, d)
        kv = jnp.concatenate([c1_ref[0, r, tq - RADIUS:rr, kv_cols], n1_ref[0, r]], axis=0)
        run_block(c1_ref[0, r, tq:rr, q_cols], kv, b1_ref, edge_mask(tq + 2 * RADIUS, False, True),
                  o1_ref, l1_ref, r + d * tq, d)
        return carry
    lax.fori_loop(0, d, res1, 0)

    d = DILATIONS[2]
    tq = b2_ref.shape[1]
    assert tt // d == tq

    def res2(r, carry):
        kv = jnp.concatenate([p2_ref[0, r], c2_ref[0, r, :, kv_cols], n2_ref[0, r]], axis=0)
        run_block(c2_ref[0, r, :, q_cols], kv, b2_ref, edge_mask(tq + 2 * RADIUS, True, True),
                  o2_ref, l2_ref, r, d)
        return carry
    lax.fori_loop(0, d, res2, 0)

    chunk = 256

    def merge(ci, carry):
        rows = pl.ds(pl.multiple_of(ci * chunk, chunk), chunk)
        for hp in range(N_PAIR):
            l0, l1, l2 = l0_ref[hp, rows, :], l1_ref[hp, rows, :], l2_ref[hp, rows, :]
            m = jnp.maximum(jnp.maximum(l0, l1), l2)
            e0, e1, e2 = jnp.exp(l0 - m), jnp.exp(l1 - m), jnp.exp(l2 - m)
            num = e0 * o0_ref[hp, rows, :] + e1 * o1_ref[hp, rows, :] + e2 * o2_ref[hp, rows, :]
            attn_ref[0, rows, hp * LANES:(hp + 1) * LANES] = (num * (1.0 / (e0 + e1 + e2))).astype(BF16)
        return carry
    lax.fori_loop(0, tt // chunk, merge, 0)


def _attention(kvq0, kvq1, kvq2):
    b, _, s, _ = kvq0.shape
    tt = T_ATTN
    nt = s // tt
    in_specs = []
    for d in DILATIONS:
        rr = tt // d
        halo_per_tile = rr // RADIUS
        n_halo = (s // d) // RADIUS
        in_specs += [
            pl.BlockSpec((1, d, rr, QKV_W), lambda bi, i: (bi, 0, i, 0)),
            pl.BlockSpec((1, d, RADIUS, 2 * ATTN_W),
                         lambda bi, i, h=halo_per_tile: (bi, 0, jnp.maximum(i * h - 1, 0), 0)),
            pl.BlockSpec((1, d, RADIUS, 2 * ATTN_W),
                         lambda bi, i, h=halo_per_tile, n=n_halo: (bi, 0, jnp.minimum((i + 1) * h, n - 1), 0)),
        ]
    tq01, tq2 = 2 * RADIUS, RADIUS
    slab = pltpu.VMEM((N_PAIR, tt, LANES), F32)
    return pl.pallas_call(
        _attn_kernel,
        grid=(b, nt),
        in_specs=in_specs,
        out_specs=pl.BlockSpec((1, tt, ATTN_W), lambda bi, i: (bi, i, 0)),
        out_shape=jax.ShapeDtypeStruct((b, s, ATTN_W), BF16),
        scratch_shapes=[
            pltpu.VMEM((HEADS, tq01, tq01 + 2 * RADIUS), F32),
            pltpu.VMEM((HEADS, tq01, tq01 + 2 * RADIUS), F32),
            pltpu.VMEM((HEADS, tq2, tq2 + 2 * RADIUS), F32),
            slab, slab, slab, slab, slab, slab,
        ],
        compiler_params=pltpu.CompilerParams(
            dimension_semantics=("arbitrary", "arbitrary"), vmem_limit_bytes=VMEM_LIMIT),
        name="attn",
    )(kvq0, kvq0, kvq0, kvq1, kvq1, kvq1, kvq2, kvq2, kvq2)


def _out_kernel(x_ref, attn_ref, za_ref, u_ref, up_ref, un_ref, pre_ref, ga_ref, gc_ref,
                cw_ref, wa_ref, wc_ref, wo_ref, fg_ref, y_ref, ubuf_ref):
    ti = pl.program_id(1)
    first, last = ti == 0, ti == pl.num_programs(1) - 1
    t = x_ref.shape[1]
    pad = ubuf_ref.shape[0] - t
    lo = pad // 2

    a_in = (za_ref[0].astype(F32) * attn_ref[0].astype(F32)).astype(BF16)
    a = jnp.dot(a_in, wa_ref[...], preferred_element_type=F32)

    ubuf_ref[0:lo, :] = jnp.where(first, 0.0, up_ref[0].astype(F32))
    ubuf_ref[lo:lo + t, :] = u_ref[0].astype(F32)
    ubuf_ref[lo + t:, :] = jnp.where(last, 0.0, un_ref[0].astype(F32))
    conv = (cw_ref[0:1, :] * ubuf_ref[lo - 1:lo - 1 + t, :]
            + cw_ref[1:2, :] * ubuf_ref[lo:lo + t, :]
            + cw_ref[2:3, :] * ubuf_ref[lo + 1:lo + 1 + t, :])
    c_in = (pre_ref[0].astype(F32) * conv).astype(BF16)
    c = jnp.dot(c_in, wc_ref[...], preferred_element_type=F32)

    merged = ga_ref[0].astype(F32) * a + gc_ref[0].astype(F32) * c
    h = x_ref[0] + jnp.dot(merged.astype(BF16), wo_ref[...], preferred_element_type=F32)
    ms = jnp.mean(h * h, axis=-1, keepdims=True)
    y_ref[0] = h * lax.rsqrt(ms + RMS_EPS) * fg_ref[...]


def _output(x, attn, za, u, pre, ga, gc, conv_w, wa_bf, wc_bf, wo_bf, final_g):
    b, s, _ = x.shape
    t = T_OUT
    halo = 8
    tok = lambda width: pl.BlockSpec((1, t, width), lambda bi, i: (bi, i, 0))
    const = lambda shape: pl.BlockSpec(shape, lambda bi, i: (0,) * len(shape))
    per_tile = t // halo
    n_halo = s // halo
    return pl.pallas_call(
        _out_kernel,
        grid=(b, s // t),
        in_specs=[
            tok(D_MODEL), tok(ATTN_W), tok(ATTN_W), tok(CONV_W),
            pl.BlockSpec((1, halo, CONV_W), lambda bi, i: (bi, jnp.maximum(i * per_tile - 1, 0), 0)),
            pl.BlockSpec((1, halo, CONV_W), lambda bi, i: (bi, jnp.minimum((i + 1) * per_tile, n_halo - 1), 0)),
            tok(CONV_W), tok(D_MODEL), tok(D_MODEL),
            const((3, CONV_W)), const((ATTN_W, D_MODEL)), const((CONV_W, D_MODEL)),
            const((D_MODEL, D_MODEL)), const((1, D_MODEL)),
        ],
        out_specs=tok(D_MODEL),
        out_shape=jax.ShapeDtypeStruct((b, s, D_MODEL), F32),
        scratch_shapes=[pltpu.VMEM((t + 2 * halo, CONV_W), F32)],
        compiler_params=pltpu.CompilerParams(
            dimension_semantics=("arbitrary", "arbitrary"), vmem_limit_bytes=VMEM_LIMIT),
        name="out",
    )(x, attn, za, u, u, u, pre, ga, gc, conv_w, wa_bf, wc_bf, wo_bf, final_g.reshape(1, D_MODEL))


def _layer_and_norm(x, norm_g, w_bf, b_gate, conv_w, wa_bf, wc_bf, wo_bf, final_g):
    kvq0, kvq1, kvq2, za, u, pre, ga, gc = _projection(x, norm_g, w_bf, b_gate)
    attn = _attention(kvq0, kvq1, kvq2)
    return _output(x, attn, za, u, pre, ga, gc, conv_w, wa_bf, wc_bf, wo_bf, final_g)


def kernel(x_prompt, x_sample, norm_g, w_in, b_gate, conv_w, w_attn_out, w_conv_out, w_o, final_g):
    assert norm_g.shape[0] == 1, "single layer"
    args = (norm_g[0], w_in[0].astype(BF16), b_gate[0], conv_w[0], w_attn_out[0].astype(BF16),
            w_conv_out[0].astype(BF16), w_o[0].astype(BF16), final_g)
    return (_layer_and_norm(x_prompt, *args), _layer_and_norm(x_sample, *args))
```

```python
import functools
import math

import jax
import jax.numpy as jnp
import numpy as np
from jax import lax
from jax.experimental import pallas as pl
from jax.experimental.pallas import tpu as pltpu

F32 = jnp.float32
BF16 = jnp.bfloat16

D_MODEL = 1024
N_GROUPS = 3
DILATIONS = (1, 4, 16)
WINDOWS = (128, 512, 2048)
HEADS = 8
HEAD_DIM = 64
ATTN_W = HEADS * HEAD_DIM
QKV_W = N_GROUPS * ATTN_W
CONV_W = 512
IN_W = 3 * QKV_W + ATTN_W + 4 * CONV_W + 2 * D_MODEL
RADIUS = 64
RMS_EPS = 1e-6
NEG_BIG = -1e30

LANES = 128
N_SLAB = D_MODEL // LANES
N_PAIR = ATTN_W // LANES

_Q0, _K0, _V0 = 0, QKV_W, 2 * QKV_W
_ZA = 3 * QKV_W
_HC = _ZA + ATTN_W
_GB = _HC + CONV_W
_GC = _GB + CONV_W
_ZB = _GC + CONV_W
_G = _ZB + CONV_W

T_PROJ = 512
T_ATTN = 1024
T_OUT = 1024
VMEM_LIMIT = 56 * 1024 * 1024


def _alibi_slopes():
    n = N_GROUPS * HEADS
    s = 2.0 ** (-8.0 * np.arange(1, n + 1) / n)
    return s.astype(np.float32).reshape(N_GROUPS, HEADS)


def _sigmoid(z):
    return 1.0 / (1.0 + jnp.exp(-z))


def _proj_kernel(x_ref, g_ref, w_ref, bg_ref,
                 kvq0_ref, kvq1_ref, kvq2_ref, za_ref, u_ref, pre_ref, ga_ref, gc_ref,
                 hslab_ref):
    t = x_ref.shape[1]
    x = x_ref[0]
    ms = jnp.mean(x * x, axis=-1, keepdims=True)
    hn = x * lax.rsqrt(ms + RMS_EPS) * g_ref[...]
    hb = hn.astype(BF16)

    def mm(h, c0, width):
        return jnp.dot(h, w_ref[:, c0:c0 + width], preferred_element_type=F32)

    kvq0_ref[0, 0, :, 0:ATTN_W] = mm(hb, _K0, ATTN_W).astype(BF16)
    kvq0_ref[0, 0, :, ATTN_W:2 * ATTN_W] = mm(hb, _V0, ATTN_W).astype(BF16)
    kvq0_ref[0, 0, :, 2 * ATTN_W:] = (mm(hb, _Q0, ATTN_W) * 0.125).astype(BF16)

    for s in range(N_SLAB):
        hslab_ref[s] = hn[:, s * LANES:(s + 1) * LANES]
    for gi, out_ref in ((1, kvq1_ref), (2, kvq2_ref)):
        d = DILATIONS[gi]
        rows = t // d
        parts = []
        for r in range(d):
            parts.append(jnp.concatenate(
                [hslab_ref[s, pl.ds(r, rows, stride=d), :] for s in range(N_SLAB)], axis=-1))
        hr = jnp.concatenate(parts, axis=0).astype(BF16)
        for c, (col, scale) in enumerate(((_K0, None), (_V0, None), (_Q0, 0.125))):
            res = mm(hr, col + gi * ATTN_W, ATTN_W)
            if scale is not None:
                res = res * scale
            res = res.astype(BF16)
            for r in range(d):
                out_ref[0, r, :, c * ATTN_W:(c + 1) * ATTN_W] = res[r * rows:(r + 1) * rows]

    za = mm(hb, _ZA, ATTN_W)
    za_ref[0] = (za * _sigmoid(za)).astype(BF16)
    u_ref[0] = (mm(hb, _GC, CONV_W) * mm(hb, _HC, CONV_W)).astype(BF16)
    zb = mm(hb, _ZB, CONV_W)
    pre_ref[0] = (zb * _sigmoid(zb) * mm(hb, _GB, CONV_W)).astype(BF16)
    ga_ref[0] = _sigmoid(mm(hb, _G, D_MODEL) + bg_ref[:, 0:D_MODEL]).astype(BF16)
    gc_ref[0] = _sigmoid(mm(hb, _G + D_MODEL, D_MODEL) + bg_ref[:, D_MODEL:]).astype(BF16)


def _projection(x, norm_g, w_bf, b_gate):
    b, s, _ = x.shape
    t = T_PROJ
    grid = (b, s // t)
    tok = lambda width: pl.BlockSpec((1, t, width), lambda bi, i: (bi, i, 0))
    res = lambda d: pl.BlockSpec((1, d, t // d, QKV_W), lambda bi, i: (bi, 0, i, 0))
    const = lambda shape: pl.BlockSpec(shape, lambda bi, i: (0,) * len(shape))
    out_shape = (
        jax.ShapeDtypeStruct((b, 1, s, QKV_W), BF16),
        jax.ShapeDtypeStruct((b, 4, s // 4, QKV_W), BF16),
        jax.ShapeDtypeStruct((b, 16, s // 16, QKV_W), BF16),
        jax.ShapeDtypeStruct((b, s, ATTN_W), BF16),
        jax.ShapeDtypeStruct((b, s, CONV_W), BF16),
        jax.ShapeDtypeStruct((b, s, CONV_W), BF16),
        jax.ShapeDtypeStruct((b, s, D_MODEL), BF16),
        jax.ShapeDtypeStruct((b, s, D_MODEL), BF16),
    )
    return pl.pallas_call(
        _proj_kernel,
        grid=grid,
        in_specs=[
            tok(D_MODEL),
            const((1, D_MODEL)),
            pl.BlockSpec((D_MODEL, IN_W), lambda bi, i: (0, 0), pipeline_mode=pl.Buffered(1)),
            const((1, 2 * D_MODEL)),
        ],
        out_specs=(res(1), res(4), res(16), tok(ATTN_W), tok(CONV_W), tok(CONV_W),
                   tok(D_MODEL), tok(D_MODEL)),
        out_shape=out_shape,
        scratch_shapes=[pltpu.VMEM((N_SLAB, t, LANES), F32)],
        compiler_params=pltpu.CompilerParams(
            dimension_semantics=("arbitrary", "arbitrary"), vmem_limit_bytes=VMEM_LIMIT),
        name="proj",
    )(x, norm_g.reshape(1, D_MODEL), w_bf, b_gate.reshape(1, 2 * D_MODEL))


def _select(cond, a, b):
    if isinstance(cond, (bool, np.bool_)):
        return a() if cond else b()
    return jnp.where(cond, a(), b())


class _Group:
    def __init__(self, gi, cur, prev, nxt, bias_ref, slabs, tt, first, last):
        self.d = DILATIONS[gi]
        self.cur, self.prev, self.nxt, self.bias_ref = cur, prev, nxt, bias_ref
        self.slabs = slabs
        self.tq = bias_ref.shape[1] // 2
        self.tk = bias_ref.shape[2]
        self.rows = tt // self.d
        self.bpr = self.rows // self.tq
        self.nsets = self.d * self.bpr
        self.first, self.last = first, last
        assert self.tk == self.tq + 2 * RADIUS and self.rows % self.tq == 0 and self.tq % RADIUS == 0

    def _split(self, n):
        if isinstance(n, int):
            return n // self.bpr, n % self.bpr
        if self.bpr == 1:
            return n, 0
        shift = self.bpr.bit_length() - 1
        assert self.bpr == 1 << shift
        return n >> shift, n & (self.bpr - 1)

    def _row(self, start, mult):
        return start if isinstance(start, int) else pl.multiple_of(start, mult)

    def q(self, n):
        r, blk = self._split(n)
        return self.cur[0, r, pl.ds(self._row(blk * self.tq, self.tq), self.tq), 2 * ATTN_W:QKV_W]

    def window(self, n, cols):
        r, blk = self._split(n)
        tq = self.tq
        at_lo = blk == 0
        at_hi = blk == self.bpr - 1
        if isinstance(blk, int):
            lo_start, hi_start = max(blk * tq - RADIUS, 0), min(blk * tq + tq, self.rows - RADIUS)
        else:
            lo_start = jnp.maximum(blk * tq - RADIUS, 0)
            hi_start = jnp.minimum(blk * tq + tq, self.rows - RADIUS)
        head = _select(at_lo, lambda: self.prev[0, r, :, cols],
                       lambda: self.cur[0, r, pl.ds(self._row(lo_start, RADIUS), RADIUS), cols])
        body = self.cur[0, r, pl.ds(self._row(blk * tq, tq), tq), cols]
        tail = _select(at_hi, lambda: self.nxt[0, r, :, cols],
                       lambda: self.cur[0, r, pl.ds(self._row(hi_start, RADIUS), RADIUS), cols])
        return jnp.concatenate([head, body, tail], axis=0)

    def col_ok(self, n):
        _, blk = self._split(n)
        col = lax.broadcasted_iota(jnp.int32, (1, self.tk), 1)
        lo_out = jnp.logical_and(self.first, blk == 0)
        hi_out = jnp.logical_and(self.last, blk == self.bpr - 1)
        bad = jnp.logical_or(jnp.logical_and(col < RADIUS, lo_out),
                             jnp.logical_and(col >= self.tk - RADIUS, hi_out))
        return jnp.logical_not(bad)

    def token_rows(self, n):
        r, blk = self._split(n)
        if self.d == 1:
            return pl.ds(self._row(blk * self.tq, self.tq), self.tq)
        return pl.ds(r + self.d * blk * self.tq, self.tq, stride=self.d)


def _attn_kernel(c0_ref, p0_ref, n0_ref, c1_ref, p1_ref, n1_ref, c2_ref, p2_ref, n2_ref,
                 attn_ref,
                 b0_ref, b1_ref, b2_ref, o1_ref, m1_ref, d1_ref, o2_ref, m2_ref, d2_ref,
                 s_scr, p_scr, st_scr):
    bi, ti = pl.program_id(0), pl.program_id(1)
    first, last = ti == 0, ti == pl.num_programs(1) - 1
    tt = attn_ref.shape[1]
    slopes = _alibi_slopes()

    @pl.when(jnp.logical_and(bi == 0, ti == 0))
    def _init_bias():
        for gi, bref in enumerate((b0_ref, b1_ref, b2_ref)):
            tq, tk = bref.shape[1] // 2, bref.shape[2]
            qi = lax.broadcasted_iota(jnp.int32, (tq, tk), 0)
            kj = lax.broadcasted_iota(jnp.int32, (tq, tk), 1)
            rel = jnp.abs(kj - RADIUS - qi)
            dist = (rel * DILATIONS[gi]).astype(F32)
            for h in range(HEADS):
                rows = slice((h % 2) * tq, (h % 2 + 1) * tq)
                bref[h // 2, rows, :] = jnp.where(rel <= RADIUS, -float(slopes[gi, h]) * dist, NEG_BIG)

    g0 = _Group(0, c0_ref, p0_ref, n0_ref, b0_ref, None, tt, first, last)
    g1 = _Group(1, c1_ref, p1_ref, n1_ref, b1_ref, (o1_ref, m1_ref, d1_ref), tt, first, last)
    g2 = _Group(2, c2_ref, p2_ref, n2_ref, b2_ref, (o2_ref, m2_ref, d2_ref), tt, first, last)
    low = lax.broadcasted_iota(jnp.int32, (1, LANES), 1) < HEAD_DIM
    zero = jnp.zeros((), BF16)

    def pick(a, tq):
        return jnp.where(low, jnp.broadcast_to(a[:tq], (tq, LANES)), jnp.broadcast_to(a[tq:], (tq, LANES)))

    def stage_scores(g, n, slot):
        q, k = g.q(n), g.window(n, slice(0, ATTN_W))
        for hp in range(N_PAIR):
            sl = slice(hp * LANES, (hp + 1) * LANES)
            qp = q[:, sl]
            q2 = jnp.concatenate([jnp.where(low, qp, zero), jnp.where(low, zero, qp)], axis=0)
            s_scr[slot, hp, 0:2 * g.tq, 0:g.tk] = lax.dot_general(
                q2, k[:, sl], (((1,), (1,)), ((), ())), preferred_element_type=F32)

    def stage_softmax(g, n, slot):
        ok = g.col_ok(n)
        rows = g.token_rows(n)
        for hp in range(N_PAIR):
            s = s_scr[slot, hp, 0:2 * g.tq, 0:g.tk] + g.bias_ref[hp]
            s = jnp.where(ok, s, NEG_BIG)
            m = jnp.max(s, axis=-1, keepdims=True)
            p = jnp.exp(s - m)
            den = jnp.sum(p, axis=-1, keepdims=True)
            p_scr[slot, hp, 0:2 * g.tq, 0:g.tk] = p.astype(BF16)
            if g.slabs is None:
                st_scr[slot, 0, hp] = pick(m, g.tq)
                st_scr[slot, 1, hp] = pick(den, g.tq)
            else:
                g.slabs[1][hp, rows, :] = pick(m, g.tq)
                g.slabs[2][hp, rows, :] = pick(den, g.tq)

    def stage_values(g, n, slot):
        v = g.window(n, slice(ATTN_W, 2 * ATTN_W))
        rows = g.token_rows(n)
        for hp in range(N_PAIR):
            sl = slice(hp * LANES, (hp + 1) * LANES)
            o = pick(jnp.dot(p_scr[slot, hp, 0:2 * g.tq, 0:g.tk], v[:, sl], preferred_element_type=F32), g.tq)
            if g.slabs is not None:
                g.slabs[0][hp, rows, :] = o
                continue
            m0, d0 = st_scr[slot, 0, hp], st_scr[slot, 1, hp]
            m1, m2 = m1_ref[hp, rows, :], m2_ref[hp, rows, :]
            m = jnp.maximum(jnp.maximum(m0, m1), m2)
            e0, e1, e2 = jnp.exp(m0 - m), jnp.exp(m1 - m), jnp.exp(m2 - m)
            num = e0 * o + e1 * o1_ref[hp, rows, :] + e2 * o2_ref[hp, rows, :]
            den = e0 * d0 + e1 * d1_ref[hp, rows, :] + e2 * d2_ref[hp, rows, :]
            attn_ref[0, rows, sl] = (num * (1.0 / den)).astype(BF16)

    order = [(g2, g2.nsets), (g1, g1.nsets), (g0, g0.nsets)]
    total = sum(cnt for _, cnt in order)

    def locate(idx):
        if idx < 0 or idx >= total:
            return None
        for g, cnt in order:
            if idx < cnt:
                return g, idx
            idx -= cnt

    def static_step(idx):
        for stage, lag in ((stage_values, 2), (stage_softmax, 1), (stage_scores, 0)):
            item = locate(idx - lag)
            if item is not None:
                stage(item[0], item[1], (idx - lag) % 2)

    base = 0
    idx = 0
    for g, cnt in order:
        assert cnt >= 3 and base % 2 == 0
        while idx < base + 2:
            static_step(idx)
            idx += 1

        def steady(n, carry, g=g):
            slot = n & 1
            stage_values(g, n - 2, slot)
            stage_softmax(g, n - 1, 1 - slot)
            stage_scores(g, n, slot)
            return carry
        lax.fori_loop(2, cnt, steady, 0)
        base += cnt
        idx = base
    static_step(total)
    static_step(total + 1)


def _attention(kvq0, kvq1, kvq2):
    b, _, s, _ = kvq0.shape
    tt = T_ATTN
    nt = s // tt
    in_specs = []
    for d in DILATIONS:
        rr = tt // d
        halo_per_tile = rr // RADIUS
        n_halo = (s // d) // RADIUS
        in_specs += [
            pl.BlockSpec((1, d, rr, QKV_W), lambda bi, i: (bi, 0, i, 0)),
            pl.BlockSpec((1, d, RADIUS, 2 * ATTN_W),
                         lambda bi, i, h=halo_per_tile: (bi, 0, jnp.maximum(i * h - 1, 0), 0)),
            pl.BlockSpec((1, d, RADIUS, 2 * ATTN_W),
                         lambda bi, i, h=halo_per_tile, n=n_halo: (bi, 0, jnp.minimum((i + 1) * h, n - 1), 0)),
        ]
    tq01, tq2 = 2 * RADIUS, RADIUS
    slab = pltpu.VMEM((N_PAIR, tt, LANES), F32)
    return pl.pallas_call(
        _attn_kernel,
        grid=(b, nt),
        in_specs=in_specs,
        out_specs=pl.BlockSpec((1, tt, ATTN_W), lambda bi, i: (bi, i, 0)),
        out_shape=jax.ShapeDtypeStruct((b, s, ATTN_W), BF16),
        scratch_shapes=[
            pltpu.VMEM((N_PAIR, 2 * tq01, tq01 + 2 * RADIUS), F32),
            pltpu.VMEM((N_PAIR, 2 * tq01, tq01 + 2 * RADIUS), F32),
            pltpu.VMEM((N_PAIR, 2 * tq2, tq2 + 2 * RADIUS), F32),
            slab, slab, slab, slab, slab, slab,
            pltpu.VMEM((2, N_PAIR, 2 * tq01, tq01 + 2 * RADIUS), F32),
            pltpu.VMEM((2, N_PAIR, 2 * tq01, tq01 + 2 * RADIUS), BF16),
            pltpu.VMEM((2, 2, N_PAIR, tq01, LANES), F32),
        ],
        compiler_params=pltpu.CompilerParams(
            dimension_semantics=("arbitrary", "arbitrary"), vmem_limit_bytes=VMEM_LIMIT),
        name="attn",
    )(kvq0, kvq0, kvq0, kvq1, kvq1, kvq1, kvq2, kvq2, kvq2)


def _out_kernel(x_ref, attn_ref, za_ref, u_ref, up_ref, un_ref, pre_ref, ga_ref, gc_ref,
                cw_ref, wa_ref, wc_ref, wo_ref, fg_ref, y_ref, ubuf_ref):
    ti = pl.program_id(1)
    first, last = ti == 0, ti == pl.num_programs(1) - 1
    t = x_ref.shape[1]
    pad = ubuf_ref.shape[0] - t
    lo = pad // 2

    a_in = (za_ref[0].astype(F32) * attn_ref[0].astype(F32)).astype(BF16)
    a = jnp.dot(a_in, wa_ref[...], preferred_element_type=F32)

    ubuf_ref[0:lo, :] = jnp.where(first, 0.0, up_ref[0].astype(F32))
    ubuf_ref[lo:lo + t, :] = u_ref[0].astype(F32)
    ubuf_ref[lo + t:, :] = jnp.where(last, 0.0, un_ref[0].astype(F32))
    conv = (cw_ref[0:1, :] * ubuf_ref[lo - 1:lo - 1 + t, :]
            + cw_ref[1:2, :] * ubuf_ref[lo:lo + t, :]
            + cw_ref[2:3, :] * ubuf_ref[lo + 1:lo + 1 + t, :])
    c_in = (pre_ref[0].astype(F32) * conv).astype(BF16)
    c = jnp.dot(c_in, wc_ref[...], preferred_element_type=F32)

    merged = ga_ref[0].astype(F32) * a + gc_ref[0].astype(F32) * c
    h = x_ref[0] + jnp.dot(merged.astype(BF16), wo_ref[...], preferred_element_type=F32)
    ms = jnp.mean(h * h, axis=-1, keepdims=True)
    y_ref[0] = h * lax.rsqrt(ms + RMS_EPS) * fg_ref[...]


def _output(x, attn, za, u, pre, ga, gc, conv_w, wa_bf, wc_bf, wo_bf, final_g):
    b, s, _ = x.shape
    t = T_OUT
    halo = 8
    tok = lambda width: pl.BlockSpec((1, t, width), lambda bi, i: (bi, i, 0))
    const = lambda shape: pl.BlockSpec(shape, lambda bi, i: (0,) * len(shape))
    per_tile = t // halo
    n_halo = s // halo
    return pl.pallas_call(
        _out_kernel,
        grid=(b, s // t),
        in_specs=[
            tok(D_MODEL), tok(ATTN_W), tok(ATTN_W), tok(CONV_W),
            pl.BlockSpec((1, halo, CONV_W), lambda bi, i: (bi, jnp.maximum(i * per_tile - 1, 0), 0)),
            pl.BlockSpec((1, halo, CONV_W), lambda bi, i: (bi, jnp.minimum((i + 1) * per_tile, n_halo - 1), 0)),
            tok(CONV_W), tok(D_MODEL), tok(D_MODEL),
            const((3, CONV_W)), const((ATTN_W, D_MODEL)), const((CONV_W, D_MODEL)),
            const((D_MODEL, D_MODEL)), const((1, D_MODEL)),
        ],
        out_specs=tok(D_MODEL),
        out_shape=jax.ShapeDtypeStruct((b, s, D_MODEL), F32),
        scratch_shapes=[pltpu.VMEM((t + 2 * halo, CONV_W), F32)],
        compiler_params=pltpu.CompilerParams(
            dimension_semantics=("arbitrary", "arbitrary"), vmem_limit_bytes=VMEM_LIMIT),
        name="out",
    )(x, attn, za, u, u, u, pre, ga, gc, conv_w, wa_bf, wc_bf, wo_bf, final_g.reshape(1, D_MODEL))


def _layer_and_norm(x, norm_g, w_bf, b_gate, conv_w, wa_bf, wc_bf, wo_bf, final_g):
    kvq0, kvq1, kvq2, za, u, pre, ga, gc = _projection(x, norm_g, w_bf, b_gate)
    attn = _attention(kvq0, kvq1, kvq2)
    return _output(x, attn, za, u, pre, ga, gc, conv_w, wa_bf, wc_bf, wo_bf, final_g)


def kernel(x_prompt, x_sample, norm_g, w_in, b_gate, conv_w, w_attn_out, w_conv_out, w_o, final_g):
    assert norm_g.shape[0] == 1, "single layer"
    args = (norm_g[0], w_in[0].astype(BF16), b_gate[0], conv_w[0], w_attn_out[0].astype(BF16),
            w_conv_out[0].astype(BF16), w_o[0].astype(BF16), final_g)
    return (_layer_and_norm(x_prompt, *args), _layer_and_norm(x_sample, *args))
```

```python
import functools
import math

import jax
import jax.numpy as jnp
import numpy as np
from jax import lax
from jax.experimental import pallas as pl
from jax.experimental.pallas import tpu as pltpu

F32 = jnp.float32
BF16 = jnp.bfloat16

D_MODEL = 1024
N_GROUPS = 3
DILATIONS = (1, 4, 16)
WINDOWS = (128, 512, 2048)
HEADS = 8
HEAD_DIM = 64
ATTN_W = HEADS * HEAD_DIM
QKV_W = N_GROUPS * ATTN_W
CONV_W = 512
IN_W = 3 * QKV_W + ATTN_W + 4 * CONV_W + 2 * D_MODEL
RADIUS = 64
RMS_EPS = 1e-6
NEG_BIG = -1e30
LOG2E = math.log2(math.e)
Q_SCALE = LOG2E / math.sqrt(HEAD_DIM)

LANES = 128
N_SLAB = D_MODEL // LANES
N_PAIR = ATTN_W // LANES

_Q0, _K0, _V0 = 0, QKV_W, 2 * QKV_W
_ZA = 3 * QKV_W
_HC = _ZA + ATTN_W
_GB = _HC + CONV_W
_GC = _GB + CONV_W
_ZB = _GC + CONV_W
_G = _ZB + CONV_W

T_PROJ = 512
T_ATTN = 1024
T_OUT = 1024
VMEM_LIMIT = 56 * 1024 * 1024


def _alibi_slopes():
    n = N_GROUPS * HEADS
    s = 2.0 ** (-8.0 * np.arange(1, n + 1) / n)
    return s.astype(np.float32).reshape(N_GROUPS, HEADS)


def _sigmoid(z):
    return 1.0 / (1.0 + jnp.exp(-z))


def _proj_kernel(x_ref, g_ref, w_ref, bg_ref,
                 kvq0_ref, kvq1_ref, kvq2_ref, za_ref, u_ref, pre_ref, ga_ref, gc_ref,
                 hslab_ref):
    t = x_ref.shape[1]
    x = x_ref[0]
    ms = jnp.mean(x * x, axis=-1, keepdims=True)
    hn = x * lax.rsqrt(ms + RMS_EPS) * g_ref[...]
    hb = hn.astype(BF16)

    def mm(h, c0, width):
        return jnp.dot(h, w_ref[:, c0:c0 + width], preferred_element_type=F32)

    kvq0_ref[0, 0, :, 0:ATTN_W] = mm(hb, _K0, ATTN_W).astype(BF16)
    kvq0_ref[0, 0, :, ATTN_W:2 * ATTN_W] = mm(hb, _V0, ATTN_W).astype(BF16)
    kvq0_ref[0, 0, :, 2 * ATTN_W:] = (mm(hb, _Q0, ATTN_W) * Q_SCALE).astype(BF16)

    for s in range(N_SLAB):
        hslab_ref[s] = hn[:, s * LANES:(s + 1) * LANES]
    for gi, out_ref in ((1, kvq1_ref), (2, kvq2_ref)):
        d = DILATIONS[gi]
        rows = t // d
        parts = []
        for r in range(d):
            parts.append(jnp.concatenate(
                [hslab_ref[s, pl.ds(r, rows, stride=d), :] for s in range(N_SLAB)], axis=-1))
        hr = jnp.concatenate(parts, axis=0).astype(BF16)
        for c, (col, scale) in enumerate(((_K0, None), (_V0, None), (_Q0, Q_SCALE))):
            res = mm(hr, col + gi * ATTN_W, ATTN_W)
            if scale is not None:
                res = res * scale
            res = res.astype(BF16)
            for r in range(d):
                out_ref[0, r, :, c * ATTN_W:(c + 1) * ATTN_W] = res[r * rows:(r + 1) * rows]

    za = mm(hb, _ZA, ATTN_W)
    za_ref[0] = (za * _sigmoid(za)).astype(BF16)
    u_ref[0] = (mm(hb, _GC, CONV_W) * mm(hb, _HC, CONV_W)).astype(BF16)
    zb = mm(hb, _ZB, CONV_W)
    pre_ref[0] = (zb * _sigmoid(zb) * mm(hb, _GB, CONV_W)).astype(BF16)
    ga_ref[0] = _sigmoid(mm(hb, _G, D_MODEL) + bg_ref[:, 0:D_MODEL]).astype(BF16)
    gc_ref[0] = _sigmoid(mm(hb, _G + D_MODEL, D_MODEL) + bg_ref[:, D_MODEL:]).astype(BF16)


def _projection(x, norm_g, w_bf, b_gate):
    b, s, _ = x.shape
    t = T_PROJ
    grid = (b, s // t)
    tok = lambda width: pl.BlockSpec((1, t, width), lambda bi, i: (bi, i, 0))
    res = lambda d: pl.BlockSpec((1, d, t // d, QKV_W), lambda bi, i: (bi, 0, i, 0))
    const = lambda shape: pl.BlockSpec(shape, lambda bi, i: (0,) * len(shape))
    out_shape = (
        jax.ShapeDtypeStruct((b, 1, s, QKV_W), BF16),
        jax.ShapeDtypeStruct((b, 4, s // 4, QKV_W), BF16),
        jax.ShapeDtypeStruct((b, 16, s // 16, QKV_W), BF16),
        jax.ShapeDtypeStruct((b, s, ATTN_W), BF16),
        jax.ShapeDtypeStruct((b, s, CONV_W), BF16),
        jax.ShapeDtypeStruct((b, s, CONV_W), BF16),
        jax.ShapeDtypeStruct((b, s, D_MODEL), BF16),
        jax.ShapeDtypeStruct((b, s, D_MODEL), BF16),
    )
    return pl.pallas_call(
        _proj_kernel,
        grid=grid,
        in_specs=[
            tok(D_MODEL),
            const((1, D_MODEL)),
            pl.BlockSpec((D_MODEL, IN_W), lambda bi, i: (0, 0), pipeline_mode=pl.Buffered(1)),
            const((1, 2 * D_MODEL)),
        ],
        out_specs=(res(1), res(4), res(16), tok(ATTN_W), tok(CONV_W), tok(CONV_W),
                   tok(D_MODEL), tok(D_MODEL)),
        out_shape=out_shape,
        scratch_shapes=[pltpu.VMEM((N_SLAB, t, LANES), F32)],
        compiler_params=pltpu.CompilerParams(
            dimension_semantics=("arbitrary", "arbitrary"), vmem_limit_bytes=VMEM_LIMIT),
        name="proj",
    )(x, norm_g.reshape(1, D_MODEL), w_bf, b_gate.reshape(1, 2 * D_MODEL))


def _select(cond, a, b):
    if isinstance(cond, (bool, np.bool_)):
        return a() if cond else b()
    return jnp.where(cond, a(), b())


class _Group:
    def __init__(self, gi, cur, prev, nxt, bias_ref, slabs, tt, first, last):
        self.d = DILATIONS[gi]
        self.cur, self.prev, self.nxt, self.bias_ref = cur, prev, nxt, bias_ref
        self.slabs = slabs
        self.tq = bias_ref.shape[1] // 2
        self.tk = bias_ref.shape[2]
        self.rows = tt // self.d
        self.bpr = self.rows // self.tq
        self.nsets = self.d * self.bpr
        self.first, self.last = first, last
        assert self.tk == self.tq + 2 * RADIUS and self.rows % self.tq == 0 and self.tq % RADIUS == 0

    def _split(self, n):
        if isinstance(n, int):
            return n // self.bpr, n % self.bpr
        if self.bpr == 1:
            return n, 0
        shift = self.bpr.bit_length() - 1
        assert self.bpr == 1 << shift
        return n >> shift, n & (self.bpr - 1)

    def _row(self, start, mult):
        return start if isinstance(start, int) else pl.multiple_of(start, mult)

    def q(self, n):
        r, blk = self._split(n)
        return self.cur[0, r, pl.ds(self._row(blk * self.tq, self.tq), self.tq), 2 * ATTN_W:QKV_W]

    def window(self, n, cols):
        r, blk = self._split(n)
        tq = self.tq
        at_lo = blk == 0
        at_hi = blk == self.bpr - 1
        if isinstance(blk, int):
            lo_start, hi_start = max(blk * tq - RADIUS, 0), min(blk * tq + tq, self.rows - RADIUS)
        else:
            lo_start = jnp.maximum(blk * tq - RADIUS, 0)
            hi_start = jnp.minimum(blk * tq + tq, self.rows - RADIUS)
        head = _select(at_lo, lambda: self.prev[0, r, :, cols],
                       lambda: self.cur[0, r, pl.ds(self._row(lo_start, RADIUS), RADIUS), cols])
        body = self.cur[0, r, pl.ds(self._row(blk * tq, tq), tq), cols]
        tail = _select(at_hi, lambda: self.nxt[0, r, :, cols],
                       lambda: self.cur[0, r, pl.ds(self._row(hi_start, RADIUS), RADIUS), cols])
        return jnp.concatenate([head, body, tail], axis=0)

    def col_ok(self, n):
        _, blk = self._split(n)
        col = lax.broadcasted_iota(jnp.int32, (1, self.tk), 1)
        lo_out = jnp.logical_and(self.first, blk == 0)
        hi_out = jnp.logical_and(self.last, blk == self.bpr - 1)
        bad = jnp.logical_or(jnp.logical_and(col < RADIUS, lo_out),
                             jnp.logical_and(col >= self.tk - RADIUS, hi_out))
        return jnp.logical_not(bad)

    def token_rows(self, n):
        r, blk = self._split(n)
        if self.d == 1:
            return pl.ds(self._row(blk * self.tq, self.tq), self.tq)
        return pl.ds(r + self.d * blk * self.tq, self.tq, stride=self.d)


def _attn_kernel(c0_ref, p0_ref, n0_ref, c1_ref, p1_ref, n1_ref, c2_ref, p2_ref, n2_ref,
                 attn_ref,
                 b0_ref, b1_ref, b2_ref, o1_ref, m1_ref, d1_ref, o2_ref, m2_ref, d2_ref,
                 s_scr, p_scr, st_scr):
    bi, ti = pl.program_id(0), pl.program_id(1)
    first, last = ti == 0, ti == pl.num_programs(1) - 1
    tt = attn_ref.shape[1]
    slopes = _alibi_slopes()

    @pl.when(jnp.logical_and(bi == 0, ti == 0))
    def _init_bias():
        for gi, bref in enumerate((b0_ref, b1_ref, b2_ref)):
            tq, tk = bref.shape[1] // 2, bref.shape[2]
            qi = lax.broadcasted_iota(jnp.int32, (tq, tk), 0)
            kj = lax.broadcasted_iota(jnp.int32, (tq, tk), 1)
            rel = jnp.abs(kj - RADIUS - qi)
            dist = (rel * DILATIONS[gi]).astype(F32)
            for h in range(HEADS):
                rows = slice((h % 2) * tq, (h % 2 + 1) * tq)
                bref[h // 2, rows, :] = jnp.where(rel <= RADIUS, (-LOG2E * float(slopes[gi, h])) * dist, NEG_BIG)

    g0 = _Group(0, c0_ref, p0_ref, n0_ref, b0_ref, None, tt, first, last)
    g1 = _Group(1, c1_ref, p1_ref, n1_ref, b1_ref, (o1_ref, m1_ref, d1_ref), tt, first, last)
    g2 = _Group(2, c2_ref, p2_ref, n2_ref, b2_ref, (o2_ref, m2_ref, d2_ref), tt, first, last)
    low = lax.broadcasted_iota(jnp.int32, (1, LANES), 1) < HEAD_DIM
    zero = jnp.zeros((), BF16)

    def pick(a, tq):
        return jnp.where(low, jnp.broadcast_to(a[:tq], (tq, LANES)), jnp.broadcast_to(a[tq:], (tq, LANES)))

    def stage_scores(g, n, slot):
        q, k = g.q(n), g.window(n, slice(0, ATTN_W))
        for hp in range(N_PAIR):
            sl = slice(hp * LANES, (hp + 1) * LANES)
            qp = q[:, sl]
            q2 = jnp.concatenate([jnp.where(low, qp, zero), jnp.where(low, zero, qp)], axis=0)
            s_scr[slot, hp, 0:2 * g.tq, 0:g.tk] = lax.dot_general(
                q2, k[:, sl], (((1,), (1,)), ((), ())), preferred_element_type=F32)

    def stage_softmax(g, n, slot):
        ok = g.col_ok(n)
        rows = g.token_rows(n)
        for hp in range(N_PAIR):
            s = s_scr[slot, hp, 0:2 * g.tq, 0:g.tk] + g.bias_ref[hp]
            s = jnp.where(ok, s, NEG_BIG)
            m = jnp.max(s, axis=-1, keepdims=True)
            p = jnp.exp2(s - m)
            den = jnp.sum(p, axis=-1, keepdims=True)
            p_scr[slot, hp, 0:2 * g.tq, 0:g.tk] = p.astype(BF16)
            if g.slabs is None:
                st_scr[slot, 0, hp] = pick(m, g.tq)
                st_scr[slot, 1, hp] = pick(den, g.tq)
            else:
                g.slabs[1][hp, rows, :] = pick(m, g.tq)
                g.slabs[2][hp, rows, :] = pick(den, g.tq)

    def stage_values(g, n, slot):
        v = g.window(n, slice(ATTN_W, 2 * ATTN_W))
        rows = g.token_rows(n)
        for hp in range(N_PAIR):
            sl = slice(hp * LANES, (hp + 1) * LANES)
            o = pick(jnp.dot(p_scr[slot, hp, 0:2 * g.tq, 0:g.tk], v[:, sl], preferred_element_type=F32), g.tq)
            if g.slabs is not None:
                g.slabs[0][hp, rows, :] = o
                continue
            m0, d0 = st_scr[slot, 0, hp], st_scr[slot, 1, hp]
            m1, m2 = m1_ref[hp, rows, :], m2_ref[hp, rows, :]
            m = jnp.maximum(jnp.maximum(m0, m1), m2)
            e0, e1, e2 = jnp.exp2(m0 - m), jnp.exp2(m1 - m), jnp.exp2(m2 - m)
            num = e0 * o + e1 * o1_ref[hp, rows, :] + e2 * o2_ref[hp, rows, :]
            den = e0 * d0 + e1 * d1_ref[hp, rows, :] + e2 * d2_ref[hp, rows, :]
            attn_ref[0, rows, sl] = (num * (1.0 / den)).astype(BF16)

    order = [(g2, g2.nsets), (g1, g1.nsets), (g0, g0.nsets)]
    total = sum(cnt for _, cnt in order)

    def locate(idx):
        if idx < 0 or idx >= total:
            return None
        for g, cnt in order:
            if idx < cnt:
                return g, idx
            idx -= cnt

    def step(g_v, n_v, g_s, n_s, g_a, n_a, slot):
        if g_v is not None:
            stage_values(g_v, n_v, slot)
        if g_s is not None:
            stage_softmax(g_s, n_s, 1 - slot)
        if g_a is not None:
            stage_scores(g_a, n_a, slot)

    def static_step(idx):
        items = [locate(idx - lag) or (None, None) for lag in (2, 1, 0)]
        step(*items[0], *items[1], *items[2], idx % 2)

    base = 0
    idx = 0
    for g, cnt in order:
        assert cnt >= 4 and cnt % 2 == 0 and base % 2 == 0
        while idx < base + 2:
            static_step(idx)
            idx += 1

        def steady(j, carry, g=g):
            n = 2 * j
            step(g, n - 2, g, n - 1, g, n, 0)
            step(g, n - 1, g, n, g, n + 1, 1)
            return carry
        lax.fori_loop(1, cnt // 2, steady, 0)
        base += cnt
        idx = base
    static_step(total)
    static_step(total + 1)


def _attention(kvq0, kvq1, kvq2):
    b, _, s, _ = kvq0.shape
    tt = T_ATTN
    nt = s // tt
    in_specs = []
    for d in DILATIONS:
        rr = tt // d
        halo_per_tile = rr // RADIUS
        n_halo = (s // d) // RADIUS
        in_specs += [
            pl.BlockSpec((1, d, rr, QKV_W), lambda bi, i: (bi, 0, i, 0)),
            pl.BlockSpec((1, d, RADIUS, 2 * ATTN_W),
                         lambda bi, i, h=halo_per_tile: (bi, 0, jnp.maximum(i * h - 1, 0), 0)),
            pl.BlockSpec((1, d, RADIUS, 2 * ATTN_W),
                         lambda bi, i, h=halo_per_tile, n=n_halo: (bi, 0, jnp.minimum((i + 1) * h, n - 1), 0)),
        ]
    tq01, tq2 = 2 * RADIUS, RADIUS
    slab = pltpu.VMEM((N_PAIR, tt, LANES), F32)
    return pl.pallas_call(
        _attn_kernel,
        grid=(b, nt),
        in_specs=in_specs,
        out_specs=pl.BlockSpec((1, tt, ATTN_W), lambda bi, i: (bi, i, 0)),
        out_shape=jax.ShapeDtypeStruct((b, s, ATTN_W), BF16),
        scratch_shapes=[
            pltpu.VMEM((N_PAIR, 2 * tq01, tq01 + 2 * RADIUS), F32),
            pltpu.VMEM((N_PAIR, 2 * tq01, tq01 + 2 * RADIUS), F32),
            pltpu.VMEM((N_PAIR, 2 * tq2, tq2 + 2 * RADIUS), F32),
            slab, slab, slab, slab, slab, slab,
            pltpu.VMEM((2, N_PAIR, 2 * tq01, tq01 + 2 * RADIUS), F32),
            pltpu.VMEM((2, N_PAIR, 2 * tq01, tq01 + 2 * RADIUS), BF16),
            pltpu.VMEM((2, 2, N_PAIR, tq01, LANES), F32),
        ],
        compiler_params=pltpu.CompilerParams(
            dimension_semantics=("arbitrary", "arbitrary"), vmem_limit_bytes=VMEM_LIMIT),
        name="attn",
    )(kvq0, kvq0, kvq0, kvq1, kvq1, kvq1, kvq2, kvq2, kvq2)


def _out_kernel(x_ref, attn_ref, za_ref, u_ref, up_ref, un_ref, pre_ref, ga_ref, gc_ref,
                cw_ref, wa_ref, wc_ref, wo_ref, fg_ref, y_ref, ubuf_ref):
    ti = pl.program_id(1)
    first, last = ti == 0, ti == pl.num_programs(1) - 1
    t = x_ref.shape[1]
    pad = ubuf_ref.shape[0] - t
    lo = pad // 2

    a_in = (za_ref[0].astype(F32) * attn_ref[0].astype(F32)).astype(BF16)
    a = jnp.dot(a_in, wa_ref[...], preferred_element_type=F32)

    ubuf_ref[0:lo, :] = jnp.where(first, 0.0, up_ref[0].astype(F32))
    ubuf_ref[lo:lo + t, :] = u_ref[0].astype(F32)
    ubuf_ref[lo + t:, :] = jnp.where(last, 0.0, un_ref[0].astype(F32))
    conv = (cw_ref[0:1, :] * ubuf_ref[lo - 1:lo - 1 + t, :]
            + cw_ref[1:2, :] * ubuf_ref[lo:lo + t, :]
            + cw_ref[2:3, :] * ubuf_ref[lo + 1:lo + 1 + t, :])
    c_in = (pre_ref[0].astype(F32) * conv).astype(BF16)
    c = jnp.dot(c_in, wc_ref[...], preferred_element_type=F32)

    merged = ga_ref[0].astype(F32) * a + gc_ref[0].astype(F32) * c
    h = x_ref[0] + jnp.dot(merged.astype(BF16), wo_ref[...], preferred_element_type=F32)
    ms = jnp.mean(h * h, axis=-1, keepdims=True)
    y_ref[0] = h * lax.rsqrt(ms + RMS_EPS) * fg_ref[...]


def _output(x, attn, za, u, pre, ga, gc, conv_w, wa_bf, wc_bf, wo_bf, final_g):
    b, s, _ = x.shape
    t = T_OUT
    halo = 8
    tok = lambda width: pl.BlockSpec((1, t, width), lambda bi, i: (bi, i, 0))
    const = lambda shape: pl.BlockSpec(shape, lambda bi, i: (0,) * len(shape))
    per_tile = t // halo
    n_halo = s // halo
    return pl.pallas_call(
        _out_kernel,
        grid=(b, s // t),
        in_specs=[
            tok(D_MODEL), tok(ATTN_W), tok(ATTN_W), tok(CONV_W),
            pl.BlockSpec((1, halo, CONV_W), lambda bi, i: (bi, jnp.maximum(i * per_tile - 1, 0), 0)),
            pl.BlockSpec((1, halo, CONV_W), lambda bi, i: (bi, jnp.minimum((i + 1) * per_tile, n_halo - 1), 0)),
            tok(CONV_W), tok(D_MODEL), tok(D_MODEL),
            const((3, CONV_W)), const((ATTN_W, D_MODEL)), const((CONV_W, D_MODEL)),
            const((D_MODEL, D_MODEL)), const((1, D_MODEL)),
        ],
        out_specs=tok(D_MODEL),
        out_shape=jax.ShapeDtypeStruct((b, s, D_MODEL), F32),
        scratch_shapes=[pltpu.VMEM((t + 2 * halo, CONV_W), F32)],
        compiler_params=pltpu.CompilerParams(
            dimension_semantics=("arbitrary", "arbitrary"), vmem_limit_bytes=VMEM_LIMIT),
        name="out",
    )(x, attn, za, u, u, u, pre, ga, gc, conv_w, wa_bf, wc_bf, wo_bf, final_g.reshape(1, D_MODEL))


def _layer_and_norm(x, norm_g, w_bf, b_gate, conv_w, wa_bf, wc_bf, wo_bf, final_g):
    kvq0, kvq1, kvq2, za, u, pre, ga, gc = _projection(x, norm_g, w_bf, b_gate)
    attn = _attention(kvq0, kvq1, kvq2)
    return _output(x, attn, za, u, pre, ga, gc, conv_w, wa_bf, wc_bf, wo_bf, final_g)


def kernel(x_prompt, x_sample, norm_g, w_in, b_gate, conv_w, w_attn_out, w_conv_out, w_o, final_g):
    assert norm_g.shape[0] == 1, "single layer"
    args = (norm_g[0], w_in[0].astype(BF16), b_gate[0], conv_w[0], w_attn_out[0].astype(BF16),
            w_conv_out[0].astype(BF16), w_o[0].astype(BF16), final_g)
    return (_layer_and_norm(x_prompt, *args), _layer_and_norm(x_sample, *args))
```

```python
import functools
import math

import jax
import jax.numpy as jnp
import numpy as np
from jax import lax
from jax.experimental import pallas as pl
from jax.experimental.pallas import tpu as pltpu

F32 = jnp.float32
BF16 = jnp.bfloat16

D_MODEL = 1024
N_GROUPS = 3
DILATIONS = (1, 4, 16)
WINDOWS = (128, 512, 2048)
HEADS = 8
HEAD_DIM = 64
ATTN_W = HEADS * HEAD_DIM
QKV_W = N_GROUPS * ATTN_W
CONV_W = 512
IN_W = 3 * QKV_W + ATTN_W + 4 * CONV_W + 2 * D_MODEL
RADIUS = 64
RMS_EPS = 1e-6
NEG_BIG = -1e30
LOG2E = math.log2(math.e)
Q_SCALE = LOG2E / math.sqrt(HEAD_DIM)

LANES = 128
N_SLAB = D_MODEL // LANES
N_PAIR = ATTN_W // LANES

_Q0, _K0, _V0 = 0, QKV_W, 2 * QKV_W
_ZA = 3 * QKV_W
_HC = _ZA + ATTN_W
_GB = _HC + CONV_W
_GC = _GB + CONV_W
_ZB = _GC + CONV_W
_G = _ZB + CONV_W

T_PROJ = 512
T_ATTN = 1024
T_OUT = 1024
VMEM_LIMIT = 56 * 1024 * 1024


def _alibi_slopes():
    n = N_GROUPS * HEADS
    s = 2.0 ** (-8.0 * np.arange(1, n + 1) / n)
    return s.astype(np.float32).reshape(N_GROUPS, HEADS)


def _sigmoid(z):
    return 1.0 / (1.0 + jnp.exp(-z))


def _proj_kernel(x_ref, g_ref, w_ref,
                 kvq0_ref, kvq1_ref, kvq2_ref, za_ref, u_ref, pre_ref, hb_ref,
                 hslab_ref):
    t = x_ref.shape[1]
    x = x_ref[0]
    ms = jnp.mean(x * x, axis=-1, keepdims=True)
    hn = x * lax.rsqrt(ms + RMS_EPS) * g_ref[...]
    hb = hn.astype(BF16)
    hb_ref[0] = hb

    def mm(h, c0, width):
        return jnp.dot(h, w_ref[:, c0:c0 + width], preferred_element_type=F32)

    kvq0_ref[0, 0, :, 0:ATTN_W] = mm(hb, _K0, ATTN_W).astype(BF16)
    kvq0_ref[0, 0, :, ATTN_W:2 * ATTN_W] = mm(hb, _V0, ATTN_W).astype(BF16)
    kvq0_ref[0, 0, :, 2 * ATTN_W:] = (mm(hb, _Q0, ATTN_W) * Q_SCALE).astype(BF16)

    for s in range(N_SLAB):
        hslab_ref[s] = hn[:, s * LANES:(s + 1) * LANES]
    for gi, out_ref in ((1, kvq1_ref), (2, kvq2_ref)):
        d = DILATIONS[gi]
        rows = t // d
        parts = []
        for r in range(d):
            parts.append(jnp.concatenate(
                [hslab_ref[s, pl.ds(r, rows, stride=d), :] for s in range(N_SLAB)], axis=-1))
        hr = jnp.concatenate(parts, axis=0).astype(BF16)
        for c, (col, scale) in enumerate(((_K0, None), (_V0, None), (_Q0, Q_SCALE))):
            res = mm(hr, col + gi * ATTN_W, ATTN_W)
            if scale is not None:
                res = res * scale
            res = res.astype(BF16)
            for r in range(d):
                out_ref[0, r, :, c * ATTN_W:(c + 1) * ATTN_W] = res[r * rows:(r + 1) * rows]

    za = mm(hb, _ZA, ATTN_W)
    za_ref[0] = (za * _sigmoid(za)).astype(BF16)
    u_ref[0] = (mm(hb, _GC, CONV_W) * mm(hb, _HC, CONV_W)).astype(BF16)
    zb = mm(hb, _ZB, CONV_W)
    pre_ref[0] = (zb * _sigmoid(zb) * mm(hb, _GB, CONV_W)).astype(BF16)


def _projection(x, norm_g, w_bf):
    b, s, _ = x.shape
    t = T_PROJ
    grid = (b, s // t)
    tok = lambda width: pl.BlockSpec((1, t, width), lambda bi, i: (bi, i, 0))
    res = lambda d: pl.BlockSpec((1, d, t // d, QKV_W), lambda bi, i: (bi, 0, i, 0))
    const = lambda shape: pl.BlockSpec(shape, lambda bi, i: (0,) * len(shape))
    out_shape = (
        jax.ShapeDtypeStruct((b, 1, s, QKV_W), BF16),
        jax.ShapeDtypeStruct((b, 4, s // 4, QKV_W), BF16),
        jax.ShapeDtypeStruct((b, 16, s // 16, QKV_W), BF16),
        jax.ShapeDtypeStruct((b, s, ATTN_W), BF16),
        jax.ShapeDtypeStruct((b, s, CONV_W), BF16),
        jax.ShapeDtypeStruct((b, s, CONV_W), BF16),
        jax.ShapeDtypeStruct((b, s, D_MODEL), BF16),
    )
    return pl.pallas_call(
        _proj_kernel,
        grid=grid,
        in_specs=[
            tok(D_MODEL),
            const((1, D_MODEL)),
            pl.BlockSpec((D_MODEL, _G), lambda bi, i: (0, 0), pipeline_mode=pl.Buffered(1)),
        ],
        out_specs=(res(1), res(4), res(16), tok(ATTN_W), tok(CONV_W), tok(CONV_W), tok(D_MODEL)),
        out_shape=out_shape,
        scratch_shapes=[pltpu.VMEM((N_SLAB, t, LANES), F32)],
        compiler_params=pltpu.CompilerParams(
            dimension_semantics=("arbitrary", "arbitrary"), vmem_limit_bytes=VMEM_LIMIT),
        name="proj",
    )(x, norm_g.reshape(1, D_MODEL), w_bf)


def _select(cond, a, b):
    if isinstance(cond, (bool, np.bool_)):
        return a() if cond else b()
    return jnp.where(cond, a(), b())


class _Group:
    def __init__(self, gi, cur, prev, nxt, bias_ref, slabs, tt, first, last):
        self.d = DILATIONS[gi]
        self.cur, self.prev, self.nxt, self.bias_ref = cur, prev, nxt, bias_ref
        self.slabs = slabs
        self.tq = bias_ref.shape[1] // 2
        self.tk = bias_ref.shape[2]
        self.rows = tt // self.d
        self.bpr = self.rows // self.tq
        self.nsets = self.d * self.bpr
        self.first, self.last = first, last
        assert self.tk == self.tq + 2 * RADIUS and self.rows % self.tq == 0 and self.tq % RADIUS == 0

    def _split(self, n):
        if isinstance(n, int):
            return n // self.bpr, n % self.bpr
        if self.bpr == 1:
            return n, 0
        shift = self.bpr.bit_length() - 1
        assert self.bpr == 1 << shift
        return n >> shift, n & (self.bpr - 1)

    def _row(self, start, mult):
        return start if isinstance(start, int) else pl.multiple_of(start, mult)

    def q(self, n):
        r, blk = self._split(n)
        return self.cur[0, r, pl.ds(self._row(blk * self.tq, self.tq), self.tq), 2 * ATTN_W:QKV_W]

    def window(self, n, cols):
        r, blk = self._split(n)
        tq = self.tq
        at_lo = blk == 0
        at_hi = blk == self.bpr - 1
        if isinstance(blk, int):
            lo_start, hi_start = max(blk * tq - RADIUS, 0), min(blk * tq + tq, self.rows - RADIUS)
        else:
            lo_start = jnp.maximum(blk * tq - RADIUS, 0)
            hi_start = jnp.minimum(blk * tq + tq, self.rows - RADIUS)
        head = _select(at_lo, lambda: self.prev[0, r, :, cols],
                       lambda: self.cur[0, r, pl.ds(self._row(lo_start, RADIUS), RADIUS), cols])
        body = self.cur[0, r, pl.ds(self._row(blk * tq, tq), tq), cols]
        tail = _select(at_hi, lambda: self.nxt[0, r, :, cols],
                       lambda: self.cur[0, r, pl.ds(self._row(hi_start, RADIUS), RADIUS), cols])
        return jnp.concatenate([head, body, tail], axis=0)

    def col_ok(self, n):
        _, blk = self._split(n)
        col = lax.broadcasted_iota(jnp.int32, (1, self.tk), 1)
        lo_out = jnp.logical_and(self.first, blk == 0)
        hi_out = jnp.logical_and(self.last, blk == self.bpr - 1)
        bad = jnp.logical_or(jnp.logical_and(col < RADIUS, lo_out),
                             jnp.logical_and(col >= self.tk - RADIUS, hi_out))
        return jnp.logical_not(bad)

    def token_rows(self, n):
        r, blk = self._split(n)
        if self.d == 1:
            return pl.ds(self._row(blk * self.tq, self.tq), self.tq)
        return pl.ds(r + self.d * blk * self.tq, self.tq, stride=self.d)


def _attn_kernel(c0_ref, p0_ref, n0_ref, c1_ref, p1_ref, n1_ref, c2_ref, p2_ref, n2_ref,
                 attn_ref,
                 b0_ref, b1_ref, b2_ref, o1_ref, m1_ref, d1_ref, o2_ref, m2_ref, d2_ref,
                 s_scr, p_scr, st_scr):
    bi, ti = pl.program_id(0), pl.program_id(1)
    first, last = ti == 0, ti == pl.num_programs(1) - 1
    tt = attn_ref.shape[1]
    slopes = _alibi_slopes()

    @pl.when(jnp.logical_and(bi == 0, ti == 0))
    def _init_bias():
        for gi, bref in enumerate((b0_ref, b1_ref, b2_ref)):
            tq, tk = bref.shape[1] // 2, bref.shape[2]
            qi = lax.broadcasted_iota(jnp.int32, (tq, tk), 0)
            kj = lax.broadcasted_iota(jnp.int32, (tq, tk), 1)
            rel = jnp.abs(kj - RADIUS - qi)
            dist = (rel * DILATIONS[gi]).astype(F32)
            for h in range(HEADS):
                rows = slice((h % 2) * tq, (h % 2 + 1) * tq)
                bref[h // 2, rows, :] = jnp.where(rel <= RADIUS, (-LOG2E * float(slopes[gi, h])) * dist, NEG_BIG)

    g0 = _Group(0, c0_ref, p0_ref, n0_ref, b0_ref, None, tt, first, last)
    g1 = _Group(1, c1_ref, p1_ref, n1_ref, b1_ref, (o1_ref, m1_ref, d1_ref), tt, first, last)
    g2 = _Group(2, c2_ref, p2_ref, n2_ref, b2_ref, (o2_ref, m2_ref, d2_ref), tt, first, last)
    low = lax.broadcasted_iota(jnp.int32, (1, LANES), 1) < HEAD_DIM
    zero = jnp.zeros((), BF16)

    def pick(a, tq):
        return jnp.where(low, jnp.broadcast_to(a[:tq], (tq, LANES)), jnp.broadcast_to(a[tq:], (tq, LANES)))

    def stage_scores(g, n, slot):
        q, k = g.q(n), g.window(n, slice(0, ATTN_W))
        for hp in range(N_PAIR):
            sl = slice(hp * LANES, (hp + 1) * LANES)
            qp = q[:, sl]
            q2 = jnp.concatenate([jnp.where(low, qp, zero), jnp.where(low, zero, qp)], axis=0)
            s_scr[slot, hp, 0:2 * g.tq, 0:g.tk] = lax.dot_general(
                q2, k[:, sl], (((1,), (1,)), ((), ())), preferred_element_type=F32)

    def stage_softmax(g, n, slot):
        ok = g.col_ok(n)
        rows = g.token_rows(n)
        for hp in range(N_PAIR):
            s = s_scr[slot, hp, 0:2 * g.tq, 0:g.tk] + g.bias_ref[hp]
            s = jnp.where(ok, s, NEG_BIG)
            m = jnp.max(s, axis=-1, keepdims=True)
            p = jnp.exp2(s - m)
            den = jnp.sum(p, axis=-1, keepdims=True)
            p_scr[slot, hp, 0:2 * g.tq, 0:g.tk] = p.astype(BF16)
            if g.slabs is None:
                st_scr[slot, 0, hp] = pick(m, g.tq)
                st_scr[slot, 1, hp] = pick(den, g.tq)
            else:
                g.slabs[1][hp, rows, :] = pick(m, g.tq)
                g.slabs[2][hp, rows, :] = pick(den, g.tq)

    def stage_values(g, n, slot):
        v = g.window(n, slice(ATTN_W, 2 * ATTN_W))
        rows = g.token_rows(n)
        for hp in range(N_PAIR):
            sl = slice(hp * LANES, (hp + 1) * LANES)
            o = pick(jnp.dot(p_scr[slot, hp, 0:2 * g.tq, 0:g.tk], v[:, sl], preferred_element_type=F32), g.tq)
            if g.slabs is not None:
                g.slabs[0][hp, rows, :] = o
                continue
            m0, d0 = st_scr[slot, 0, hp], st_scr[slot, 1, hp]
            m1, m2 = m1_ref[hp, rows, :], m2_ref[hp, rows, :]
            m = jnp.maximum(jnp.maximum(m0, m1), m2)
            e0, e1, e2 = jnp.exp2(m0 - m), jnp.exp2(m1 - m), jnp.exp2(m2 - m)
            num = e0 * o + e1 * o1_ref[hp, rows, :] + e2 * o2_ref[hp, rows, :]
            den = e0 * d0 + e1 * d1_ref[hp, rows, :] + e2 * d2_ref[hp, rows, :]
            attn_ref[0, rows, sl] = (num * (1.0 / den)).astype(BF16)

    order = [(g2, g2.nsets), (g1, g1.nsets), (g0, g0.nsets)]
    total = sum(cnt for _, cnt in order)

    def locate(idx):
        if idx < 0 or idx >= total:
            return None
        for g, cnt in order:
            if idx < cnt:
                return g, idx
            idx -= cnt

    def step(g_v, n_v, g_s, n_s, g_a, n_a, slot):
        if g_v is not None:
            stage_values(g_v, n_v, slot)
        if g_s is not None:
            stage_softmax(g_s, n_s, 1 - slot)
        if g_a is not None:
            stage_scores(g_a, n_a, slot)

    def static_step(idx):
        items = [locate(idx - lag) or (None, None) for lag in (2, 1, 0)]
        step(*items[0], *items[1], *items[2], idx % 2)

    base = 0
    idx = 0
    for g, cnt in order:
        assert cnt >= 4 and cnt % 2 == 0 and base % 2 == 0
        while idx < base + 2:
            static_step(idx)
            idx += 1

        def steady(j, carry, g=g):
            n = 2 * j
            step(g, n - 2, g, n - 1, g, n, 0)
            step(g, n - 1, g, n, g, n + 1, 1)
            return carry
        lax.fori_loop(1, cnt // 2, steady, 0)
        base += cnt
        idx = base
    static_step(total)
    static_step(total + 1)


def _attention(kvq0, kvq1, kvq2):
    b, _, s, _ = kvq0.shape
    tt = T_ATTN
    nt = s // tt
    in_specs = []
    for d in DILATIONS:
        rr = tt // d
        halo_per_tile = rr // RADIUS
        n_halo = (s // d) // RADIUS
        in_specs += [
            pl.BlockSpec((1, d, rr, QKV_W), lambda bi, i: (bi, 0, i, 0)),
            pl.BlockSpec((1, d, RADIUS, 2 * ATTN_W),
                         lambda bi, i, h=halo_per_tile: (bi, 0, jnp.maximum(i * h - 1, 0), 0)),
            pl.BlockSpec((1, d, RADIUS, 2 * ATTN_W),
                         lambda bi, i, h=halo_per_tile, n=n_halo: (bi, 0, jnp.minimum((i + 1) * h, n - 1), 0)),
        ]
    tq01, tq2 = 2 * RADIUS, RADIUS
    slab = pltpu.VMEM((N_PAIR, tt, LANES), F32)
    return pl.pallas_call(
        _attn_kernel,
        grid=(b, nt),
        in_specs=in_specs,
        out_specs=pl.BlockSpec((1, tt, ATTN_W), lambda bi, i: (bi, i, 0)),
        out_shape=jax.ShapeDtypeStruct((b, s, ATTN_W), BF16),
        scratch_shapes=[
            pltpu.VMEM((N_PAIR, 2 * tq01, tq01 + 2 * RADIUS), F32),
            pltpu.VMEM((N_PAIR, 2 * tq01, tq01 + 2 * RADIUS), F32),
            pltpu.VMEM((N_PAIR, 2 * tq2, tq2 + 2 * RADIUS), F32),
            slab, slab, slab, slab, slab, slab,
            pltpu.VMEM((2, N_PAIR, 2 * tq01, tq01 + 2 * RADIUS), F32),
            pltpu.VMEM((2, N_PAIR, 2 * tq01, tq01 + 2 * RADIUS), BF16),
            pltpu.VMEM((2, 2, N_PAIR, tq01, LANES), F32),
        ],
        compiler_params=pltpu.CompilerParams(
            dimension_semantics=("arbitrary", "arbitrary"), vmem_limit_bytes=VMEM_LIMIT),
        name="attn",
    )(kvq0, kvq0, kvq0, kvq1, kvq1, kvq1, kvq2, kvq2, kvq2)


def _out_kernel(x_ref, attn_ref, za_ref, u_ref, up_ref, un_ref, pre_ref, hb_ref,
                cw_ref, wa_ref, wc_ref, wga_ref, wgc_ref, bg_ref, wo_ref, fg_ref, y_ref, ubuf_ref):
    ti = pl.program_id(1)
    first, last = ti == 0, ti == pl.num_programs(1) - 1
    t = x_ref.shape[1]
    pad = ubuf_ref.shape[0] - t
    lo = pad // 2

    a_in = (za_ref[0].astype(F32) * attn_ref[0].astype(F32)).astype(BF16)
    a = jnp.dot(a_in, wa_ref[...], preferred_element_type=F32)

    ubuf_ref[0:lo, :] = jnp.where(first, 0.0, up_ref[0].astype(F32))
    ubuf_ref[lo:lo + t, :] = u_ref[0].astype(F32)
    ubuf_ref[lo + t:, :] = jnp.where(last, 0.0, un_ref[0].astype(F32))
    conv = (cw_ref[0:1, :] * ubuf_ref[lo - 1:lo - 1 + t, :]
            + cw_ref[1:2, :] * ubuf_ref[lo:lo + t, :]
            + cw_ref[2:3, :] * ubuf_ref[lo + 1:lo + 1 + t, :])
    c_in = (pre_ref[0].astype(F32) * conv).astype(BF16)
    c = jnp.dot(c_in, wc_ref[...], preferred_element_type=F32)

    hb = hb_ref[0]
    gate_a = _sigmoid(jnp.dot(hb, wga_ref[...], preferred_element_type=F32) + bg_ref[:, 0:D_MODEL])
    gate_c = _sigmoid(jnp.dot(hb, wgc_ref[...], preferred_element_type=F32) + bg_ref[:, D_MODEL:])
    merged = gate_a * a + gate_c * c
    h = x_ref[0] + jnp.dot(merged.astype(BF16), wo_ref[...], preferred_element_type=F32)
    ms = jnp.mean(h * h, axis=-1, keepdims=True)
    y_ref[0] = h * lax.rsqrt(ms + RMS_EPS) * fg_ref[...]


def _output(x, attn, za, u, pre, hb, conv_w, w_bf, b_gate, wa_bf, wc_bf, wo_bf, final_g):
    b, s, _ = x.shape
    t = T_OUT
    halo = 8
    tok = lambda width: pl.BlockSpec((1, t, width), lambda bi, i: (bi, i, 0))
    const = lambda shape: pl.BlockSpec(shape, lambda bi, i: (0,) * len(shape), pipeline_mode=pl.Buffered(1))
    gate_w = lambda j: pl.BlockSpec((D_MODEL, D_MODEL), lambda bi, i: (0, _G // D_MODEL + j),
                                    pipeline_mode=pl.Buffered(1))
    assert _G % D_MODEL == 0
    per_tile = t // halo
    n_halo = s // halo
    return pl.pallas_call(
        _out_kernel,
        grid=(b, s // t),
        in_specs=[
            tok(D_MODEL), tok(ATTN_W), tok(ATTN_W), tok(CONV_W),
            pl.BlockSpec((1, halo, CONV_W), lambda bi, i: (bi, jnp.maximum(i * per_tile - 1, 0), 0)),
            pl.BlockSpec((1, halo, CONV_W), lambda bi, i: (bi, jnp.minimum((i + 1) * per_tile, n_halo - 1), 0)),
            tok(CONV_W), tok(D_MODEL),
            const((3, CONV_W)), const((ATTN_W, D_MODEL)), const((CONV_W, D_MODEL)),
            gate_w(0), gate_w(1), const((1, 2 * D_MODEL)),
            const((D_MODEL, D_MODEL)), const((1, D_MODEL)),
        ],
        out_specs=tok(D_MODEL),
        out_shape=jax.ShapeDtypeStruct((b, s, D_MODEL), F32),
        scratch_shapes=[pltpu.VMEM((t + 2 * halo, CONV_W), F32)],
        compiler_params=pltpu.CompilerParams(
            dimension_semantics=("arbitrary", "arbitrary"), vmem_limit_bytes=VMEM_LIMIT),
        name="out",
    )(x, attn, za, u, u, u, pre, hb, conv_w, wa_bf, wc_bf, w_bf, w_bf, b_gate.reshape(1, 2 * D_MODEL),
      wo_bf, final_g.reshape(1, D_MODEL))


def _layer_and_norm(x, norm_g, w_bf, b_gate, conv_w, wa_bf, wc_bf, wo_bf, final_g):
    kvq0, kvq1, kvq2, za, u, pre, hb = _projection(x, norm_g, w_bf)
    attn = _attention(kvq0, kvq1, kvq2)
    return _output(x, attn, za, u, pre, hb, conv_w, w_bf, b_gate, wa_bf, wc_bf, wo_bf, final_g)


def kernel(x_prompt, x_sample, norm_g, w_in, b_gate, conv_w, w_attn_out, w_conv_out, w_o, final_g):
    assert norm_g.shape[0] == 1, "single layer"
    args = (norm_g[0], w_in[0].astype(BF16), b_gate[0], conv_w[0], w_attn_out[0].astype(BF16),
            w_conv_out[0].astype(BF16), w_o[0].astype(BF16), final_g)
    return (_layer_and_norm(x_prompt, *args), _layer_and_norm(x_sample, *args))
```

```python
import functools
import math

import jax
import jax.numpy as jnp
import numpy as np
from jax import lax
from jax.experimental import pallas as pl
from jax.experimental.pallas import tpu as pltpu

F32 = jnp.float32
BF16 = jnp.bfloat16

D_MODEL = 1024
N_GROUPS = 3
DILATIONS = (1, 4, 16)
WINDOWS = (128, 512, 2048)
HEADS = 8
HEAD_DIM = 64
ATTN_W = HEADS * HEAD_DIM
QKV_W = N_GROUPS * ATTN_W
CONV_W = 512
IN_W = 3 * QKV_W + ATTN_W + 4 * CONV_W + 2 * D_MODEL
RADIUS = 64
RMS_EPS = 1e-6
NEG_BIG = -1e30
LOG2E = math.log2(math.e)
Q_SCALE = LOG2E / math.sqrt(HEAD_DIM)

LANES = 128
N_SLAB = D_MODEL // LANES
N_PAIR = ATTN_W // LANES

_Q0, _K0, _V0 = 0, QKV_W, 2 * QKV_W
_ZA = 3 * QKV_W
_HC = _ZA + ATTN_W
_GB = _HC + CONV_W
_GC = _GB + CONV_W
_ZB = _GC + CONV_W
_G = _ZB + CONV_W

T_PROJ = 512
T_ATTN = 1024
T_OUT = 512
VMEM_LIMIT = 56 * 1024 * 1024


def _alibi_slopes():
    n = N_GROUPS * HEADS
    s = 2.0 ** (-8.0 * np.arange(1, n + 1) / n)
    return s.astype(np.float32).reshape(N_GROUPS, HEADS)


def _sigmoid(z):
    return 1.0 / (1.0 + jnp.exp(-z))


def _proj_kernel(x_ref, g_ref, w_ref,
                 kvq0_ref, kvq1_ref, kvq2_ref, za_ref, u_ref, pre_ref, hb_ref,
                 hslab_ref):
    t = x_ref.shape[1]
    x = x_ref[0]
    ms = jnp.mean(x * x, axis=-1, keepdims=True)
    hn = x * lax.rsqrt(ms + RMS_EPS) * g_ref[...]
    hb = hn.astype(BF16)
    hb_ref[0] = hb

    def mm(h, c0, width):
        return jnp.dot(h, w_ref[:, c0:c0 + width], preferred_element_type=F32)

    kvq0_ref[0, 0, :, 0:ATTN_W] = mm(hb, _K0, ATTN_W).astype(BF16)
    kvq0_ref[0, 0, :, ATTN_W:2 * ATTN_W] = mm(hb, _V0, ATTN_W).astype(BF16)
    kvq0_ref[0, 0, :, 2 * ATTN_W:] = (mm(hb, _Q0, ATTN_W) * Q_SCALE).astype(BF16)

    for s in range(N_SLAB):
        hslab_ref[s] = hn[:, s * LANES:(s + 1) * LANES]
    for gi, out_ref in ((1, kvq1_ref), (2, kvq2_ref)):
        d = DILATIONS[gi]
        rows = t // d
        parts = []
        for r in range(d):
            parts.append(jnp.concatenate(
                [hslab_ref[s, pl.ds(r, rows, stride=d), :] for s in range(N_SLAB)], axis=-1))
        hr = jnp.concatenate(parts, axis=0).astype(BF16)
        for c, (col, scale) in enumerate(((_K0, None), (_V0, None), (_Q0, Q_SCALE))):
            res = mm(hr, col + gi * ATTN_W, ATTN_W)
            if scale is not None:
                res = res * scale
            res = res.astype(BF16)
            for r in range(d):
                out_ref[0, r, :, c * ATTN_W:(c + 1) * ATTN_W] = res[r * rows:(r + 1) * rows]

    za = mm(hb, _ZA, ATTN_W)
    za_ref[0] = (za * _sigmoid(za)).astype(BF16)
    u_ref[0] = (mm(hb, _GC, CONV_W) * mm(hb, _HC, CONV_W)).astype(BF16)
    zb = mm(hb, _ZB, CONV_W)
    pre_ref[0] = (zb * _sigmoid(zb) * mm(hb, _GB, CONV_W)).astype(BF16)


def _projection(x, norm_g, w_bf):
    b, s, _ = x.shape
    t = T_PROJ
    grid = (b, s // t)
    tok = lambda width: pl.BlockSpec((1, t, width), lambda bi, i: (bi, i, 0))
    res = lambda d: pl.BlockSpec((1, d, t // d, QKV_W), lambda bi, i: (bi, 0, i, 0))
    const = lambda shape: pl.BlockSpec(shape, lambda bi, i: (0,) * len(shape))
    out_shape = (
        jax.ShapeDtypeStruct((b, 1, s, QKV_W), BF16),
        jax.ShapeDtypeStruct((b, 4, s // 4, QKV_W), BF16),
        jax.ShapeDtypeStruct((b, 16, s // 16, QKV_W), BF16),
        jax.ShapeDtypeStruct((b, s, ATTN_W), BF16),
        jax.ShapeDtypeStruct((b, s, CONV_W), BF16),
        jax.ShapeDtypeStruct((b, s, CONV_W), BF16),
        jax.ShapeDtypeStruct((b, s, D_MODEL), BF16),
    )
    return pl.pallas_call(
        _proj_kernel,
        grid=grid,
        in_specs=[
            tok(D_MODEL),
            const((1, D_MODEL)),
            pl.BlockSpec((D_MODEL, _G), lambda bi, i: (0, 0), pipeline_mode=pl.Buffered(1)),
        ],
        out_specs=(res(1), res(4), res(16), tok(ATTN_W), tok(CONV_W), tok(CONV_W), tok(D_MODEL)),
        out_shape=out_shape,
        scratch_shapes=[pltpu.VMEM((N_SLAB, t, LANES), F32)],
        compiler_params=pltpu.CompilerParams(
            dimension_semantics=("arbitrary", "arbitrary"), vmem_limit_bytes=VMEM_LIMIT),
        name="proj",
    )(x, norm_g.reshape(1, D_MODEL), w_bf)


def _select(cond, a, b):
    if isinstance(cond, (bool, np.bool_)):
        return a() if cond else b()
    return jnp.where(cond, a(), b())


class _Group:
    def __init__(self, gi, cur, prev, nxt, bias_ref, tt, first, last):
        self.d = DILATIONS[gi]
        self.cur, self.prev, self.nxt, self.bias_ref = cur, prev, nxt, bias_ref
        self.tq = bias_ref.shape[1] // 2
        self.tk = bias_ref.shape[2]
        self.rows = tt // self.d
        self.bpr = self.rows // self.tq
        self.nsets = self.d * self.bpr
        self.first, self.last = first, last
        assert self.tk == self.tq + 2 * RADIUS and self.rows % self.tq == 0 and self.tq % RADIUS == 0

    def _split(self, n):
        if isinstance(n, int):
            return n // self.bpr, n % self.bpr
        if self.bpr == 1:
            return n, 0
        shift = self.bpr.bit_length() - 1
        assert self.bpr == 1 << shift
        return n >> shift, n & (self.bpr - 1)

    def _row(self, start, mult):
        return start if isinstance(start, int) else pl.multiple_of(start, mult)

    def q(self, n):
        r, blk = self._split(n)
        return self.cur[0, r, pl.ds(self._row(blk * self.tq, self.tq), self.tq), 2 * ATTN_W:QKV_W]

    def window(self, n, cols):
        r, blk = self._split(n)
        tq = self.tq
        at_lo = blk == 0
        at_hi = blk == self.bpr - 1
        if isinstance(blk, int):
            lo_start, hi_start = max(blk * tq - RADIUS, 0), min(blk * tq + tq, self.rows - RADIUS)
        else:
            lo_start = jnp.maximum(blk * tq - RADIUS, 0)
            hi_start = jnp.minimum(blk * tq + tq, self.rows - RADIUS)
        head = _select(at_lo, lambda: self.prev[0, r, :, cols],
                       lambda: self.cur[0, r, pl.ds(self._row(lo_start, RADIUS), RADIUS), cols])
        body = self.cur[0, r, pl.ds(self._row(blk * tq, tq), tq), cols]
        tail = _select(at_hi, lambda: self.nxt[0, r, :, cols],
                       lambda: self.cur[0, r, pl.ds(self._row(hi_start, RADIUS), RADIUS), cols])
        return jnp.concatenate([head, body, tail], axis=0)

    def col_ok(self, n):
        _, blk = self._split(n)
        col = lax.broadcasted_iota(jnp.int32, (1, self.tk), 1)
        lo_out = jnp.logical_and(self.first, blk == 0)
        hi_out = jnp.logical_and(self.last, blk == self.bpr - 1)
        bad = jnp.logical_or(jnp.logical_and(col < RADIUS, lo_out),
                             jnp.logical_and(col >= self.tk - RADIUS, hi_out))
        return jnp.logical_not(bad)

    def token_rows(self, n):
        r, blk = self._split(n)
        if self.d == 1:
            return pl.ds(self._row(blk * self.tq, self.tq), self.tq)
        return pl.ds(r + self.d * blk * self.tq, self.tq, stride=self.d)


def _bias_table(bias_ref, gi):
    slopes = _alibi_slopes()
    tq, tk = bias_ref.shape[1] // 2, bias_ref.shape[2]
    qi = lax.broadcasted_iota(jnp.int32, (tq, tk), 0)
    kj = lax.broadcasted_iota(jnp.int32, (tq, tk), 1)
    rel = jnp.abs(kj - RADIUS - qi)
    dist = (rel * DILATIONS[gi]).astype(F32)
    for h in range(HEADS):
        rows = slice((h % 2) * tq, (h % 2 + 1) * tq)
        bias_ref[h // 2, rows, :] = jnp.where(rel <= RADIUS, (-LOG2E * float(slopes[gi, h])) * dist, NEG_BIG)


def _make_stages(s_scr, p_scr, st_scr):
    low = lax.broadcasted_iota(jnp.int32, (1, LANES), 1) < HEAD_DIM
    zero = jnp.zeros((), BF16)

    def pick(a, tq):
        return jnp.where(low, jnp.broadcast_to(a[:tq], (tq, LANES)), jnp.broadcast_to(a[tq:], (tq, LANES)))

    def scores(g, n, slot):
        q, k = g.q(n), g.window(n, slice(0, ATTN_W))
        for hp in range(N_PAIR):
            sl = slice(hp * LANES, (hp + 1) * LANES)
            qp = q[:, sl]
            q2 = jnp.concatenate([jnp.where(low, qp, zero), jnp.where(low, zero, qp)], axis=0)
            s_scr[slot, hp, 0:2 * g.tq, 0:g.tk] = lax.dot_general(
                q2, k[:, sl], (((1,), (1,)), ((), ())), preferred_element_type=F32)

    def softmax(g, n, slot):
        ok = g.col_ok(n)
        for hp in range(N_PAIR):
            s = s_scr[slot, hp, 0:2 * g.tq, 0:g.tk] + g.bias_ref[hp]
            s = jnp.where(ok, s, NEG_BIG)
            m = jnp.max(s, axis=-1, keepdims=True)
            p = jnp.exp2(s - m)
            den = jnp.sum(p, axis=-1, keepdims=True)
            p_scr[slot, hp, 0:2 * g.tq, 0:g.tk] = p.astype(BF16)
            st_scr[slot, 0, hp, 0:g.tq] = pick(m, g.tq)
            st_scr[slot, 1, hp, 0:g.tq] = pick(den, g.tq)

    def values(g, n, slot):
        v = g.window(n, slice(ATTN_W, 2 * ATTN_W))
        for hp in range(N_PAIR):
            o = jnp.dot(p_scr[slot, hp, 0:2 * g.tq, 0:g.tk], v[:, hp * LANES:(hp + 1) * LANES],
                        preferred_element_type=F32)
            yield hp, pick(o, g.tq), st_scr[slot, 0, hp, 0:g.tq], st_scr[slot, 1, hp, 0:g.tq]

    return scores, softmax, values


def _attn12_kernel(c1_ref, p1_ref, n1_ref, c2_ref, p2_ref, n2_ref,
                   o1_ref, l1_ref, o2_ref, l2_ref,
                   b1_ref, b2_ref, s_scr, p_scr, st_scr):
    bi, ti = pl.program_id(0), pl.program_id(1)
    first, last = ti == 0, ti == pl.num_programs(1) - 1
    tt = o1_ref.shape[2]

    @pl.when(jnp.logical_and(bi == 0, ti == 0))
    def _init_bias():
        _bias_table(b1_ref, 1)
        _bias_table(b2_ref, 2)

    g1 = _Group(1, c1_ref, p1_ref, n1_ref, b1_ref, tt, first, last)
    g2 = _Group(2, c2_ref, p2_ref, n2_ref, b2_ref, tt, first, last)
    scores, softmax, values = _make_stages(s_scr, p_scr, st_scr)

    def finish(g, n, slot):
        rows = g.token_rows(n)
        o_ref, l_ref = (o1_ref, l1_ref) if g is g1 else (o2_ref, l2_ref)
        for hp, o, m, d in values(g, n, slot):
            o_ref[0, hp, rows, :] = o * (1.0 / d)
            l_ref[0, hp, rows, :] = m + jnp.log2(d)

    order = [(g2, g2.nsets), (g1, g1.nsets)]
    total = sum(cnt for _, cnt in order)

    def locate(idx):
        if idx < 0 or idx >= total:
            return None
        for g, cnt in order:
            if idx < cnt:
                return g, idx
            idx -= cnt

    def step(g_v, n_v, g_s, n_s, g_a, n_a, slot):
        if g_v is not None:
            finish(g_v, n_v, slot)
        if g_s is not None:
            softmax(g_s, n_s, 1 - slot)
        if g_a is not None:
            scores(g_a, n_a, slot)

    def static_step(idx):
        items = [locate(idx - lag) or (None, None) for lag in (2, 1, 0)]
        step(*items[0], *items[1], *items[2], idx % 2)

    base = 0
    idx = 0
    for g, cnt in order:
        assert cnt >= 4 and cnt % 2 == 0 and base % 2 == 0
        while idx < base + 2:
            static_step(idx)
            idx += 1

        def steady(j, carry, g=g):
            n = 2 * j
            step(g, n - 2, g, n - 1, g, n, 0)
            step(g, n - 1, g, n, g, n + 1, 1)
            return carry
        lax.fori_loop(1, cnt // 2, steady, 0)
        base += cnt
        idx = base
    static_step(total)
    static_step(total + 1)


def _halo_specs(d, rows_per_tile, seq_rows, width):
    per_tile = rows_per_tile // RADIUS
    n_halo = seq_rows // RADIUS
    return [
        pl.BlockSpec((1, d, RADIUS, width), lambda bi, i: (bi, 0, jnp.maximum(i * per_tile - 1, 0), 0)),
        pl.BlockSpec((1, d, RADIUS, width), lambda bi, i: (bi, 0, jnp.minimum((i + 1) * per_tile, n_halo - 1), 0)),
    ]


def _attention12(kvq1, kvq2):
    b, _, s4, _ = kvq1.shape
    s = s4 * DILATIONS[1]
    tt = T_ATTN
    in_specs = []
    for d in DILATIONS[1:]:
        in_specs += [pl.BlockSpec((1, d, tt // d, QKV_W), lambda bi, i: (bi, 0, i, 0))]
        in_specs += _halo_specs(d, tt // d, s // d, 2 * ATTN_W)
    tq1, tq2 = 2 * RADIUS, RADIUS
    out_spec = pl.BlockSpec((1, N_PAIR, tt, LANES), lambda bi, i: (bi, 0, i, 0))
    out_sds = jax.ShapeDtypeStruct((b, N_PAIR, s, LANES), F32)
    return pl.pallas_call(
        _attn12_kernel,
        grid=(b, s // tt),
        in_specs=in_specs,
        out_specs=(out_spec,) * 4,
        out_shape=(out_sds,) * 4,
        scratch_shapes=[
            pltpu.VMEM((N_PAIR, 2 * tq1, tq1 + 2 * RADIUS), F32),
            pltpu.VMEM((N_PAIR, 2 * tq2, tq2 + 2 * RADIUS), F32),
            pltpu.VMEM((2, N_PAIR, 2 * tq1, tq1 + 2 * RADIUS), F32),
            pltpu.VMEM((2, N_PAIR, 2 * tq1, tq1 + 2 * RADIUS), BF16),
            pltpu.VMEM((2, 2, N_PAIR, tq1, LANES), F32),
        ],
        compiler_params=pltpu.CompilerParams(
            dimension_semantics=("arbitrary", "arbitrary"), vmem_limit_bytes=VMEM_LIMIT),
        name="attn12",
    )(kvq1, kvq1, kvq1, kvq2, kvq2, kvq2)


def _out_kernel(x_ref, c0_ref, p0_ref, n0_ref, o1_ref, l1_ref, o2_ref, l2_ref,
                za_ref, u_ref, up_ref, un_ref, pre_ref, hb_ref,
                cw_ref, wa_ref, wc_ref, wga_ref, wgc_ref, bg_ref, wo_ref, fg_ref, y_ref,
                ubuf_ref, b0_ref, s_scr, p_scr, st_scr, attn_scr):
    bi, ti = pl.program_id(0), pl.program_id(1)
    first, last = ti == 0, ti == pl.num_programs(1) - 1
    t = x_ref.shape[1]
    pad = ubuf_ref.shape[0] - t
    lo = pad // 2

    @pl.when(jnp.logical_and(bi == 0, ti == 0))
    def _init_bias():
        _bias_table(b0_ref, 0)

    g0 = _Group(0, c0_ref, p0_ref, n0_ref, b0_ref, t, first, last)
    scores, softmax, values = _make_stages(s_scr, p_scr, st_scr)

    def finish(n, slot):
        rows = g0.token_rows(n)
        for hp, o, m, d in values(g0, n, slot):
            l1, l2 = l1_ref[0, hp, rows, :], l2_ref[0, hp, rows, :]
            top = jnp.maximum(jnp.maximum(m, l1), l2)
            e0, e1, e2 = jnp.exp2(m - top), jnp.exp2(l1 - top), jnp.exp2(l2 - top)
            num = e0 * o + e1 * o1_ref[0, hp, rows, :] + e2 * o2_ref[0, hp, rows, :]
            attn_scr[rows, hp * LANES:(hp + 1) * LANES] = num * (1.0 / (e0 * d + e1 + e2))

    hb = hb_ref[0]
    half = D_MODEL // 2

    def gate_piece(w_ref, j):
        cols = slice(j * half, (j + 1) * half)
        return jnp.dot(hb, w_ref[:, cols], preferred_element_type=F32)

    def conv_branch():
        ubuf_ref[0:lo, :] = jnp.where(first, 0.0, up_ref[0].astype(F32))
        ubuf_ref[lo:lo + t, :] = u_ref[0].astype(F32)
        ubuf_ref[lo + t:, :] = jnp.where(last, 0.0, un_ref[0].astype(F32))
        conv = (cw_ref[0:1, :] * ubuf_ref[lo - 1:lo - 1 + t, :]
                + cw_ref[1:2, :] * ubuf_ref[lo:lo + t, :]
                + cw_ref[2:3, :] * ubuf_ref[lo + 1:lo + 1 + t, :])
        c_in = (pre_ref[0].astype(F32) * conv).astype(BF16)
        return jnp.dot(c_in, wc_ref[...], preferred_element_type=F32)

    fillers = [lambda: gate_piece(wga_ref, 0), lambda: gate_piece(wga_ref, 1),
               lambda: gate_piece(wgc_ref, 0), lambda: gate_piece(wgc_ref, 1), conv_branch]
    filled = []
    assert g0.nsets + 2 >= len(fillers)
    for n in range(g0.nsets + 2):
        if n >= 2:
            finish(n - 2, n % 2)
        if 1 <= n <= g0.nsets:
            softmax(g0, n - 1, 1 - n % 2)
        if n < g0.nsets:
            scores(g0, n, n % 2)
        if n < len(fillers):
            filled.append(fillers[n]())
    ga0, ga1, gc0, gc1, c = filled
    gate_a = _sigmoid(jnp.concatenate([ga0, ga1], axis=1) + bg_ref[:, 0:D_MODEL])
    gate_c = _sigmoid(jnp.concatenate([gc0, gc1], axis=1) + bg_ref[:, D_MODEL:])

    a_in = (za_ref[0].astype(F32) * attn_scr[...]).astype(BF16)
    a = jnp.dot(a_in, wa_ref[...], preferred_element_type=F32)
    merged = gate_a * a + gate_c * c
    h = x_ref[0] + jnp.dot(merged.astype(BF16), wo_ref[...], preferred_element_type=F32)
    ms = jnp.mean(h * h, axis=-1, keepdims=True)
    y_ref[0] = h * lax.rsqrt(ms + RMS_EPS) * fg_ref[...]


def _output(x, kvq0, groups12, za, u, pre, hb, conv_w, w_bf, b_gate, wa_bf, wc_bf, wo_bf, final_g):
    b, s, _ = x.shape
    t = T_OUT
    halo = 8
    tq0 = 2 * RADIUS
    tok = lambda width: pl.BlockSpec((1, t, width), lambda bi, i: (bi, i, 0))
    slabs = pl.BlockSpec((1, N_PAIR, t, LANES), lambda bi, i: (bi, 0, i, 0))
    const = lambda shape: pl.BlockSpec(shape, lambda bi, i: (0,) * len(shape), pipeline_mode=pl.Buffered(1))
    gate_w = lambda j: pl.BlockSpec((D_MODEL, D_MODEL), lambda bi, i: (0, _G // D_MODEL + j),
                                    pipeline_mode=pl.Buffered(1))
    assert _G % D_MODEL == 0
    per_tile = t // halo
    n_halo = s // halo
    return pl.pallas_call(
        _out_kernel,
        grid=(b, s // t),
        in_specs=[
            tok(D_MODEL),
            pl.BlockSpec((1, 1, t, QKV_W), lambda bi, i: (bi, 0, i, 0)), *_halo_specs(1, t, s, 2 * ATTN_W),
            slabs, slabs, slabs, slabs,
            tok(ATTN_W), tok(CONV_W),
            pl.BlockSpec((1, halo, CONV_W), lambda bi, i: (bi, jnp.maximum(i * per_tile - 1, 0), 0)),
            pl.BlockSpec((1, halo, CONV_W), lambda bi, i: (bi, jnp.minimum((i + 1) * per_tile, n_halo - 1), 0)),
            tok(CONV_W), tok(D_MODEL),
            const((3, CONV_W)), const((ATTN_W, D_MODEL)), const((CONV_W, D_MODEL)),
            gate_w(0), gate_w(1), const((1, 2 * D_MODEL)),
            const((D_MODEL, D_MODEL)), const((1, D_MODEL)),
        ],
        out_specs=tok(D_MODEL),
        out_shape=jax.ShapeDtypeStruct((b, s, D_MODEL), F32),
        scratch_shapes=[
            pltpu.VMEM((t + 2 * halo, CONV_W), F32),
            pltpu.VMEM((N_PAIR, 2 * tq0, tq0 + 2 * RADIUS), F32),
            pltpu.VMEM((2, N_PAIR, 2 * tq0, tq0 + 2 * RADIUS), F32),
            pltpu.VMEM((2, N_PAIR, 2 * tq0, tq0 + 2 * RADIUS), BF16),
            pltpu.VMEM((2, 2, N_PAIR, tq0, LANES), F32),
            pltpu.VMEM((t, ATTN_W), F32),
        ],
        compiler_params=pltpu.CompilerParams(
            dimension_semantics=("arbitrary", "arbitrary"), vmem_limit_bytes=VMEM_LIMIT),
        name="out",
    )(x, kvq0, kvq0, kvq0, *groups12, za, u, u, u, pre, hb, conv_w, wa_bf, wc_bf, w_bf, w_bf,
      b_gate.reshape(1, 2 * D_MODEL), wo_bf, final_g.reshape(1, D_MODEL))


def _layer_and_norm(x, norm_g, w_bf, b_gate, conv_w, wa_bf, wc_bf, wo_bf, final_g):
    kvq0, kvq1, kvq2, za, u, pre, hb = _projection(x, norm_g, w_bf)
    groups12 = _attention12(kvq1, kvq2)
    return _output(x, kvq0, groups12, za, u, pre, hb, conv_w, w_bf, b_gate, wa_bf, wc_bf, wo_bf, final_g)


def kernel(x_prompt, x_sample, norm_g, w_in, b_gate, conv_w, w_attn_out, w_conv_out, w_o, final_g):
    assert norm_g.shape[0] == 1, "single layer"
    args = (norm_g[0], w_in[0].astype(BF16), b_gate[0], conv_w[0], w_attn_out[0].astype(BF16),
            w_conv_out[0].astype(BF16), w_o[0].astype(BF16), final_g)
    return (_layer_and_norm(x_prompt, *args), _layer_and_norm(x_sample, *args))
```

```python
import functools
import math

import jax
import jax.numpy as jnp
import numpy as np
from jax import lax
from jax.experimental import pallas as pl
from jax.experimental.pallas import tpu as pltpu

F32 = jnp.float32
BF16 = jnp.bfloat16

D_MODEL = 1024
N_GROUPS = 3
DILATIONS = (1, 4, 16)
WINDOWS = (128, 512, 2048)
HEADS = 8
HEAD_DIM = 64
ATTN_W = HEADS * HEAD_DIM
QKV_W = N_GROUPS * ATTN_W
CONV_W = 512
IN_W = 3 * QKV_W + ATTN_W + 4 * CONV_W + 2 * D_MODEL
RADIUS = 64
RMS_EPS = 1e-6
NEG_BIG = -1e30
LOG2E = math.log2(math.e)
Q_SCALE = LOG2E / math.sqrt(HEAD_DIM)

LANES = 128
N_SLAB = D_MODEL // LANES
N_PAIR = ATTN_W // LANES

_Q0, _K0, _V0 = 0, QKV_W, 2 * QKV_W
_ZA = 3 * QKV_W
_HC = _ZA + ATTN_W
_GB = _HC + CONV_W
_GC = _GB + CONV_W
_ZB = _GC + CONV_W
_G = _ZB + CONV_W

T_PROJ = 512
T_ATTN = 2048
T_OUT = 512
VMEM_LIMIT = 56 * 1024 * 1024


def _alibi_slopes():
    n = N_GROUPS * HEADS
    s = 2.0 ** (-8.0 * np.arange(1, n + 1) / n)
    return s.astype(np.float32).reshape(N_GROUPS, HEADS)


def _sigmoid(z):
    return 1.0 / (1.0 + jnp.exp(-z))


def _proj_kernel(x_ref, g_ref, w_ref,
                 kvq0_ref, kvq1_ref, kvq2_ref, za_ref, u_ref, pre_ref, hb_ref,
                 hslab_ref):
    t = x_ref.shape[1]
    x = x_ref[0]
    ms = jnp.mean(x * x, axis=-1, keepdims=True)
    hn = x * lax.rsqrt(ms + RMS_EPS) * g_ref[...]
    hb = hn.astype(BF16)
    hb_ref[0] = hb

    def mm(h, c0, width):
        return jnp.dot(h, w_ref[:, c0:c0 + width], preferred_element_type=F32)

    kvq0_ref[0, 0, :, 0:ATTN_W] = mm(hb, _K0, ATTN_W).astype(BF16)
    kvq0_ref[0, 0, :, ATTN_W:2 * ATTN_W] = mm(hb, _V0, ATTN_W).astype(BF16)
    kvq0_ref[0, 0, :, 2 * ATTN_W:] = (mm(hb, _Q0, ATTN_W) * Q_SCALE).astype(BF16)

    for s in range(N_SLAB):
        hslab_ref[s] = hn[:, s * LANES:(s + 1) * LANES]
    for gi, out_ref in ((1, kvq1_ref), (2, kvq2_ref)):
        d = DILATIONS[gi]
        rows = t // d
        parts = []
        for r in range(d):
            parts.append(jnp.concatenate(
                [hslab_ref[s, pl.ds(r, rows, stride=d), :] for s in range(N_SLAB)], axis=-1))
        hr = jnp.concatenate(parts, axis=0).astype(BF16)
        for c, (col, scale) in enumerate(((_K0, None), (_V0, None), (_Q0, Q_SCALE))):
            res = mm(hr, col + gi * ATTN_W, ATTN_W)
            if scale is not None:
                res = res * scale
            res = res.astype(BF16)
            for r in range(d):
                out_ref[0, r, :, c * ATTN_W:(c + 1) * ATTN_W] = res[r * rows:(r + 1) * rows]

    za = mm(hb, _ZA, ATTN_W)
    za_ref[0] = (za * _sigmoid(za)).astype(BF16)
    u_ref[0] = (mm(hb, _GC, CONV_W) * mm(hb, _HC, CONV_W)).astype(BF16)
    zb = mm(hb, _ZB, CONV_W)
    pre_ref[0] = (zb * _sigmoid(zb) * mm(hb, _GB, CONV_W)).astype(BF16)


def _projection(x, norm_g, w_bf):
    b, s, _ = x.shape
    t = T_PROJ
    grid = (b, s // t)
    tok = lambda width: pl.BlockSpec((1, t, width), lambda bi, i: (bi, i, 0))
    res = lambda d: pl.BlockSpec((1, d, t // d, QKV_W), lambda bi, i: (bi, 0, i, 0))
    const = lambda shape: pl.BlockSpec(shape, lambda bi, i: (0,) * len(shape))
    out_shape = (
        jax.ShapeDtypeStruct((b, 1, s, QKV_W), BF16),
        jax.ShapeDtypeStruct((b, 4, s // 4, QKV_W), BF16),
        jax.ShapeDtypeStruct((b, 16, s // 16, QKV_W), BF16),
        jax.ShapeDtypeStruct((b, s, ATTN_W), BF16),
        jax.ShapeDtypeStruct((b, s, CONV_W), BF16),
        jax.ShapeDtypeStruct((b, s, CONV_W), BF16),
        jax.ShapeDtypeStruct((b, s, D_MODEL), BF16),
    )
    return pl.pallas_call(
        _proj_kernel,
        grid=grid,
        in_specs=[
            tok(D_MODEL),
            const((1, D_MODEL)),
            pl.BlockSpec((D_MODEL, _G), lambda bi, i: (0, 0), pipeline_mode=pl.Buffered(1)),
        ],
        out_specs=(res(1), res(4), res(16), tok(ATTN_W), tok(CONV_W), tok(CONV_W), tok(D_MODEL)),
        out_shape=out_shape,
        scratch_shapes=[pltpu.VMEM((N_SLAB, t, LANES), F32)],
        compiler_params=pltpu.CompilerParams(
            dimension_semantics=("arbitrary", "arbitrary"), vmem_limit_bytes=VMEM_LIMIT),
        name="proj",
    )(x, norm_g.reshape(1, D_MODEL), w_bf)


def _select(cond, a, b):
    if isinstance(cond, (bool, np.bool_)):
        return a() if cond else b()
    return jnp.where(cond, a(), b())


class _Group:
    def __init__(self, gi, cur, prev, nxt, bias_ref, tt, first, last, minor=1):
        self.d = DILATIONS[gi]
        self.minor = minor
        self.cur, self.prev, self.nxt, self.bias_ref = cur, prev, nxt, bias_ref
        self.tq = bias_ref.shape[1] // 2
        self.tk = bias_ref.shape[2]
        self.rows = tt // self.d
        self.bpr = self.rows // self.tq
        self.nsets = self.d * self.bpr
        self.first, self.last = first, last
        assert self.tk == self.tq + 2 * RADIUS and self.rows % self.tq == 0 and self.tq % RADIUS == 0

    def _split(self, n):
        if self.minor != 1:
            assert self.bpr == 1
            per = self.d // self.minor
            if isinstance(n, int):
                return self.minor * (n % per) + n // per, 0
            shift = per.bit_length() - 1
            assert per == 1 << shift
            return self.minor * (n & (per - 1)) + (n >> shift), 0
        if isinstance(n, int):
            return n // self.bpr, n % self.bpr
        if self.bpr == 1:
            return n, 0
        shift = self.bpr.bit_length() - 1
        assert self.bpr == 1 << shift
        return n >> shift, n & (self.bpr - 1)

    def _row(self, start, mult):
        return start if isinstance(start, int) else pl.multiple_of(start, mult)

    def q(self, n):
        r, blk = self._split(n)
        return self.cur[0, r, pl.ds(self._row(blk * self.tq, self.tq), self.tq), 2 * ATTN_W:QKV_W]

    def window(self, n, cols):
        r, blk = self._split(n)
        tq = self.tq
        at_lo = blk == 0
        at_hi = blk == self.bpr - 1
        if isinstance(blk, int):
            lo_start, hi_start = max(blk * tq - RADIUS, 0), min(blk * tq + tq, self.rows - RADIUS)
        else:
            lo_start = jnp.maximum(blk * tq - RADIUS, 0)
            hi_start = jnp.minimum(blk * tq + tq, self.rows - RADIUS)
        head = _select(at_lo, lambda: self.prev[0, r, :, cols],
                       lambda: self.cur[0, r, pl.ds(self._row(lo_start, RADIUS), RADIUS), cols])
        body = self.cur[0, r, pl.ds(self._row(blk * tq, tq), tq), cols]
        tail = _select(at_hi, lambda: self.nxt[0, r, :, cols],
                       lambda: self.cur[0, r, pl.ds(self._row(hi_start, RADIUS), RADIUS), cols])
        return jnp.concatenate([head, body, tail], axis=0)

    def col_ok(self, n):
        _, blk = self._split(n)
        col = lax.broadcasted_iota(jnp.int32, (1, self.tk), 1)
        lo_out = jnp.logical_and(self.first, blk == 0)
        hi_out = jnp.logical_and(self.last, blk == self.bpr - 1)
        bad = jnp.logical_or(jnp.logical_and(col < RADIUS, lo_out),
                             jnp.logical_and(col >= self.tk - RADIUS, hi_out))
        return jnp.logical_not(bad)

    def token_rows(self, n):
        r, blk = self._split(n)
        if self.d == 1:
            return pl.ds(self._row(blk * self.tq, self.tq), self.tq)
        return pl.ds(r + self.d * blk * self.tq, self.tq, stride=self.d)


def _bias_table(bias_ref, gi):
    slopes = _alibi_slopes()
    tq, tk = bias_ref.shape[1] // 2, bias_ref.shape[2]
    qi = lax.broadcasted_iota(jnp.int32, (tq, tk), 0)
    kj = lax.broadcasted_iota(jnp.int32, (tq, tk), 1)
    rel = jnp.abs(kj - RADIUS - qi)
    dist = (rel * DILATIONS[gi]).astype(F32)
    for h in range(HEADS):
        rows = slice((h % 2) * tq, (h % 2 + 1) * tq)
        bias_ref[h // 2, rows, :] = jnp.where(rel <= RADIUS, (-LOG2E * float(slopes[gi, h])) * dist, NEG_BIG)


def _make_stages(s_scr, p_scr, st_scr):
    low = lax.broadcasted_iota(jnp.int32, (1, LANES), 1) < HEAD_DIM
    zero = jnp.zeros((), BF16)

    def pick(a, tq):
        return jnp.where(low, jnp.broadcast_to(a[:tq], (tq, LANES)), jnp.broadcast_to(a[tq:], (tq, LANES)))

    def scores(g, n, slot):
        q, k = g.q(n), g.window(n, slice(0, ATTN_W))
        for hp in range(N_PAIR):
            sl = slice(hp * LANES, (hp + 1) * LANES)
            qp = q[:, sl]
            q2 = jnp.concatenate([jnp.where(low, qp, zero), jnp.where(low, zero, qp)], axis=0)
            s_scr[slot, hp, 0:2 * g.tq, 0:g.tk] = lax.dot_general(
                q2, k[:, sl], (((1,), (1,)), ((), ())), preferred_element_type=F32)

    def softmax(g, n, slot):
        ok = g.col_ok(n)
        for hp in range(N_PAIR):
            s = s_scr[slot, hp, 0:2 * g.tq, 0:g.tk] + g.bias_ref[hp]
            s = jnp.where(ok, s, NEG_BIG)
            m = jnp.max(s, axis=-1, keepdims=True)
            p = jnp.exp2(s - m)
            den = jnp.sum(p, axis=-1, keepdims=True)
            p_scr[slot, hp, 0:2 * g.tq, 0:g.tk] = p.astype(BF16)
            st_scr[slot, 0, hp, 0:g.tq] = pick(m, g.tq)
            st_scr[slot, 1, hp, 0:g.tq] = pick(den, g.tq)

    def values(g, n, slot):
        v = g.window(n, slice(ATTN_W, 2 * ATTN_W))
        for hp in range(N_PAIR):
            o = jnp.dot(p_scr[slot, hp, 0:2 * g.tq, 0:g.tk], v[:, hp * LANES:(hp + 1) * LANES],
                        preferred_element_type=F32)
            yield hp, pick(o, g.tq), st_scr[slot, 0, hp, 0:g.tq], st_scr[slot, 1, hp, 0:g.tq]

    return scores, softmax, values


def _attn_group_kernel(cur_ref, prev_ref, nxt_ref, o_ref, l_ref, bias_ref, s_scr, p_scr, st_scr, *rest, gi):
    bi, ti = pl.program_id(0), pl.program_id(1)
    first, last = ti == 0, ti == pl.num_programs(1) - 1
    tt = o_ref.shape[2]
    d = DILATIONS[gi]
    unroll = 4
    two_hop = d == unroll * unroll
    (stage_ref,) = rest if two_hop else (None,)

    @pl.when(jnp.logical_and(bi == 0, ti == 0))
    def _init_bias():
        _bias_table(bias_ref, gi)

    g = _Group(gi, cur_ref, prev_ref, nxt_ref, bias_ref, tt, first, last, minor=unroll if two_hop else 1)
    scores, softmax, values = _make_stages(s_scr, p_scr, st_scr)

    def finish(n, phase):
        slot = phase % 2
        if not two_hop:
            rows = g.token_rows(n)
            for hp, o, m, den in values(g, n, slot):
                o_ref[0, hp, rows, :] = o * (1.0 / den)
                l_ref[0, hp, rows, :] = m + jnp.log2(den)
            return
        chunk = n // unroll if isinstance(n, int) else n >> 2
        buf = chunk % 2 if isinstance(chunk, int) else chunk & 1
        rows = pl.ds(phase, g.tq, stride=unroll)
        for hp, o, m, den in values(g, n, slot):
            stage_ref[buf, 0, hp, rows, :] = o * (1.0 / den)
            stage_ref[buf, 1, hp, rows, :] = m + jnp.log2(den)
        if phase == unroll - 1:
            spread = pl.ds(chunk, unroll * g.tq, stride=unroll)
            for hp in range(N_PAIR):
                o_ref[0, hp, spread, :] = stage_ref[buf, 0, hp]
                l_ref[0, hp, spread, :] = stage_ref[buf, 1, hp]

    def step(n, phase):
        static = isinstance(n, int)
        if not static or n >= 2:
            finish(n - 2, (phase - 2) % unroll)
        if not static or 1 <= n <= g.nsets:
            softmax(g, n - 1, (phase - 1) % 2)
        if not static or n < g.nsets:
            scores(g, n, phase % 2)

    assert g.nsets >= 2 * unroll and g.nsets % unroll == 0 and (not two_hop or g.tq * d == tt)
    for n in range(unroll):
        step(n, n)

    def steady(k, carry):
        for phase in range(unroll):
            step(unroll * k + phase, phase)
        return carry
    lax.fori_loop(1, g.nsets // unroll, steady, 0)
    for n in range(g.nsets, g.nsets + 2):
        step(n, n % unroll)


def _halo_specs(d, rows_per_tile, seq_rows, width):
    per_tile = rows_per_tile // RADIUS
    n_halo = seq_rows // RADIUS
    return [
        pl.BlockSpec((1, d, RADIUS, width), lambda bi, i: (bi, 0, jnp.maximum(i * per_tile - 1, 0), 0)),
        pl.BlockSpec((1, d, RADIUS, width), lambda bi, i: (bi, 0, jnp.minimum((i + 1) * per_tile, n_halo - 1), 0)),
    ]


def _attention_group(kvq, gi):
    b, d, rows, _ = kvq.shape
    s = rows * d
    tt = T_ATTN
    tq = 2 * RADIUS
    out_spec = pl.BlockSpec((1, N_PAIR, tt, LANES), lambda bi, i: (bi, 0, i, 0))
    out_sds = jax.ShapeDtypeStruct((b, N_PAIR, s, LANES), F32)
    return pl.pallas_call(
        functools.partial(_attn_group_kernel, gi=gi),
        grid=(b, s // tt),
        in_specs=[pl.BlockSpec((1, d, tt // d, QKV_W), lambda bi, i: (bi, 0, i, 0)),
                  *_halo_specs(d, tt // d, rows, 2 * ATTN_W)],
        out_specs=(out_spec, out_spec),
        out_shape=(out_sds, out_sds),
        scratch_shapes=[
            pltpu.VMEM((N_PAIR, 2 * tq, tq + 2 * RADIUS), F32),
            pltpu.VMEM((2, N_PAIR, 2 * tq, tq + 2 * RADIUS), F32),
            pltpu.VMEM((2, N_PAIR, 2 * tq, tq + 2 * RADIUS), BF16),
            pltpu.VMEM((2, 2, N_PAIR, tq, LANES), F32),
        ] + ([pltpu.VMEM((2, 2, N_PAIR, 4 * tq, LANES), F32)] if d == 16 else []),
        compiler_params=pltpu.CompilerParams(
            dimension_semantics=("arbitrary", "arbitrary"), vmem_limit_bytes=VMEM_LIMIT),
        name=f"attn{gi}",
    )(kvq, kvq, kvq)


def _out_kernel(x_ref, c0_ref, p0_ref, n0_ref, o1_ref, l1_ref, o2_ref, l2_ref,
                za_ref, u_ref, up_ref, un_ref, pre_ref, hb_ref,
                cw_ref, wa_ref, wc_ref, wga_ref, wgc_ref, bg_ref, wo_ref, fg_ref, y_ref,
                ubuf_ref, b0_ref, s_scr, p_scr, st_scr, attn_scr):
    bi, ti = pl.program_id(0), pl.program_id(1)
    first, last = ti == 0, ti == pl.num_programs(1) - 1
    t = x_ref.shape[1]
    pad = ubuf_ref.shape[0] - t
    lo = pad // 2

    @pl.when(jnp.logical_and(bi == 0, ti == 0))
    def _init_bias():
        _bias_table(b0_ref, 0)

    g0 = _Group(0, c0_ref, p0_ref, n0_ref, b0_ref, t, first, last)
    scores, softmax, values = _make_stages(s_scr, p_scr, st_scr)

    def finish(n, slot):
        rows = g0.token_rows(n)
        for hp, o, m, d in values(g0, n, slot):
            l1, l2 = l1_ref[0, hp, rows, :], l2_ref[0, hp, rows, :]
            top = jnp.maximum(jnp.maximum(m, l1), l2)
            e0, e1, e2 = jnp.exp2(m - top), jnp.exp2(l1 - top), jnp.exp2(l2 - top)
            num = e0 * o + e1 * o1_ref[0, hp, rows, :] + e2 * o2_ref[0, hp, rows, :]
            attn_scr[rows, hp * LANES:(hp + 1) * LANES] = num * (1.0 / (e0 * d + e1 + e2))

    hb = hb_ref[0]
    half = D_MODEL // 2

    def gate_piece(w_ref, j):
        cols = slice(j * half, (j + 1) * half)
        return jnp.dot(hb, w_ref[:, cols], preferred_element_type=F32)

    def conv_branch():
        ubuf_ref[0:lo, :] = jnp.where(first, 0.0, up_ref[0].astype(F32))
        ubuf_ref[lo:lo + t, :] = u_ref[0].astype(F32)
        ubuf_ref[lo + t:, :] = jnp.where(last, 0.0, un_ref[0].astype(F32))
        conv = (cw_ref[0:1, :] * ubuf_ref[lo - 1:lo - 1 + t, :]
                + cw_ref[1:2, :] * ubuf_ref[lo:lo + t, :]
                + cw_ref[2:3, :] * ubuf_ref[lo + 1:lo + 1 + t, :])
        c_in = (pre_ref[0].astype(F32) * conv).astype(BF16)
        return jnp.dot(c_in, wc_ref[...], preferred_element_type=F32)

    fillers = [lambda: gate_piece(wga_ref, 0), lambda: gate_piece(wga_ref, 1),
               lambda: gate_piece(wgc_ref, 0), lambda: gate_piece(wgc_ref, 1), conv_branch]
    filled = []
    assert g0.nsets + 2 >= len(fillers)
    for n in range(g0.nsets + 2):
        if n >= 2:
            finish(n - 2, n % 2)
        if 1 <= n <= g0.nsets:
            softmax(g0, n - 1, 1 - n % 2)
        if n < g0.nsets:
            scores(g0, n, n % 2)
        if n < len(fillers):
            filled.append(fillers[n]())
    ga0, ga1, gc0, gc1, c = filled
    gate_a = _sigmoid(jnp.concatenate([ga0, ga1], axis=1) + bg_ref[:, 0:D_MODEL])
    gate_c = _sigmoid(jnp.concatenate([gc0, gc1], axis=1) + bg_ref[:, D_MODEL:])

    a_in = (za_ref[0].astype(F32) * attn_scr[...]).astype(BF16)
    a = jnp.dot(a_in, wa_ref[...], preferred_element_type=F32)
    merged = gate_a * a + gate_c * c
    h = x_ref[0] + jnp.dot(merged.astype(BF16), wo_ref[...], preferred_element_type=F32)
    ms = jnp.mean(h * h, axis=-1, keepdims=True)
    y_ref[0] = h * lax.rsqrt(ms + RMS_EPS) * fg_ref[...]


def _output(x, kvq0, groups12, za, u, pre, hb, conv_w, w_bf, b_gate, wa_bf, wc_bf, wo_bf, final_g):
    b, s, _ = x.shape
    t = T_OUT
    halo = 8
    tq0 = 2 * RADIUS
    tok = lambda width: pl.BlockSpec((1, t, width), lambda bi, i: (bi, i, 0))
    slabs = pl.BlockSpec((1, N_PAIR, t, LANES), lambda bi, i: (bi, 0, i, 0))
    const = lambda shape: pl.BlockSpec(shape, lambda bi, i: (0,) * len(shape), pipeline_mode=pl.Buffered(1))
    gate_w = lambda j: pl.BlockSpec((D_MODEL, D_MODEL), lambda bi, i: (0, _G // D_MODEL + j),
                                    pipeline_mode=pl.Buffered(1))
    assert _G % D_MODEL == 0
    per_tile = t // halo
    n_halo = s // halo
    return pl.pallas_call(
        _out_kernel,
        grid=(b, s // t),
        in_specs=[
            tok(D_MODEL),
            pl.BlockSpec((1, 1, t, QKV_W), lambda bi, i: (bi, 0, i, 0)), *_halo_specs(1, t, s, 2 * ATTN_W),
            slabs, slabs, slabs, slabs,
            tok(ATTN_W), tok(CONV_W),
            pl.BlockSpec((1, halo, CONV_W), lambda bi, i: (bi, jnp.maximum(i * per_tile - 1, 0), 0)),
            pl.BlockSpec((1, halo, CONV_W), lambda bi, i: (bi, jnp.minimum((i + 1) * per_tile, n_halo - 1), 0)),
            tok(CONV_W), tok(D_MODEL),
            const((3, CONV_W)), const((ATTN_W, D_MODEL)), const((CONV_W, D_MODEL)),
            gate_w(0), gate_w(1), const((1, 2 * D_MODEL)),
            const((D_MODEL, D_MODEL)), const((1, D_MODEL)),
        ],
        out_specs=tok(D_MODEL),
        out_shape=jax.ShapeDtypeStruct((b, s, D_MODEL), F32),
        scratch_shapes=[
            pltpu.VMEM((t + 2 * halo, CONV_W), F32),
            pltpu.VMEM((N_PAIR, 2 * tq0, tq0 + 2 * RADIUS), F32),
            pltpu.VMEM((2, N_PAIR, 2 * tq0, tq0 + 2 * RADIUS), F32),
            pltpu.VMEM((2, N_PAIR, 2 * tq0, tq0 + 2 * RADIUS), BF16),
            pltpu.VMEM((2, 2, N_PAIR, tq0, LANES), F32),
            pltpu.VMEM((t, ATTN_W), F32),
        ],
        compiler_params=pltpu.CompilerParams(
            dimension_semantics=("arbitrary", "arbitrary"), vmem_limit_bytes=VMEM_LIMIT),
        name="out",
    )(x, kvq0, kvq0, kvq0, *groups12, za, u, u, u, pre, hb, conv_w, wa_bf, wc_bf, w_bf, w_bf,
      b_gate.reshape(1, 2 * D_MODEL), wo_bf, final_g.reshape(1, D_MODEL))


def _layer_and_norm(x, norm_g, w_bf, b_gate, conv_w, wa_bf, wc_bf, wo_bf, final_g):
    kvq0, kvq1, kvq2, za, u, pre, hb = _projection(x, norm_g, w_bf)
    groups12 = (*_attention_group(kvq1, 1), *_attention_group(kvq2, 2))
    return _output(x, kvq0, groups12, za, u, pre, hb, conv_w, w_bf, b_gate, wa_bf, wc_bf, wo_bf, final_g)


def kernel(x_prompt, x_sample, norm_g, w_in, b_gate, conv_w, w_attn_out, w_conv_out, w_o, final_g):
    assert norm_g.shape[0] == 1, "single layer"
    args = (norm_g[0], w_in[0].astype(BF16), b_gate[0], conv_w[0], w_attn_out[0].astype(BF16),
            w_conv_out[0].astype(BF16), w_o[0].astype(BF16), final_g)
    return (_layer_and_norm(x_prompt, *args), _layer_and_norm(x_sample, *args))
```

```python
import functools
import math

import jax
import jax.numpy as jnp
import numpy as np
from jax import lax
from jax.experimental import pallas as pl
from jax.experimental.pallas import tpu as pltpu

F32 = jnp.float32
BF16 = jnp.bfloat16

D_MODEL = 1024
N_GROUPS = 3
DILATIONS = (1, 4, 16)
WINDOWS = (128, 512, 2048)
HEADS = 8
HEAD_DIM = 64
ATTN_W = HEADS * HEAD_DIM
QKV_W = N_GROUPS * ATTN_W
CONV_W = 512
IN_W = 3 * QKV_W + ATTN_W + 4 * CONV_W + 2 * D_MODEL
RADIUS = 64
RMS_EPS = 1e-6
NEG_BIG = -1e30
LOG2E = math.log2(math.e)
Q_SCALE = LOG2E / math.sqrt(HEAD_DIM)

LANES = 128
N_SLAB = D_MODEL // LANES
N_PAIR = ATTN_W // LANES

_Q0, _K0, _V0 = 0, QKV_W, 2 * QKV_W
_ZA = 3 * QKV_W
_HC = _ZA + ATTN_W
_GB = _HC + CONV_W
_GC = _GB + CONV_W
_ZB = _GC + CONV_W
_G = _ZB + CONV_W

T_PROJ = 512
T_ATTN = 2048
T_OUT = 512
VMEM_LIMIT = 56 * 1024 * 1024


def _alibi_slopes():
    n = N_GROUPS * HEADS
    s = 2.0 ** (-8.0 * np.arange(1, n + 1) / n)
    return s.astype(np.float32).reshape(N_GROUPS, HEADS)


def _sigmoid(z):
    return 1.0 / (1.0 + jnp.exp(-z))


def _proj_kernel(x_ref, g_ref, w_ref,
                 kvq0_ref, kvq1_ref, kvq2_ref, za_ref, u_ref, pre_ref, hb_ref,
                 hslab_ref):
    t = x_ref.shape[1]
    x = x_ref[0]
    ms = jnp.mean(x * x, axis=-1, keepdims=True)
    hn = x * lax.rsqrt(ms + RMS_EPS) * g_ref[...]
    hb = hn.astype(BF16)
    hb_ref[0] = hb

    def mm(h, c0, width):
        return jnp.dot(h, w_ref[:, c0:c0 + width], preferred_element_type=F32)

    kvq0_ref[0, 0, :, 0:ATTN_W] = mm(hb, _K0, ATTN_W).astype(BF16)
    kvq0_ref[0, 0, :, ATTN_W:2 * ATTN_W] = mm(hb, _V0, ATTN_W).astype(BF16)
    kvq0_ref[0, 0, :, 2 * ATTN_W:] = (mm(hb, _Q0, ATTN_W) * Q_SCALE).astype(BF16)

    for s in range(N_SLAB):
        hslab_ref[s] = hn[:, s * LANES:(s + 1) * LANES]
    for gi, out_ref in ((1, kvq1_ref), (2, kvq2_ref)):
        d = DILATIONS[gi]
        rows = t // d
        parts = []
        for r in range(d):
            parts.append(jnp.concatenate(
                [hslab_ref[s, pl.ds(r, rows, stride=d), :] for s in range(N_SLAB)], axis=-1))
        hr = jnp.concatenate(parts, axis=0).astype(BF16)
        for c, (col, scale) in enumerate(((_K0, None), (_V0, None), (_Q0, Q_SCALE))):
            res = mm(hr, col + gi * ATTN_W, ATTN_W)
            if scale is not None:
                res = res * scale
            res = res.astype(BF16)
            for r in range(d):
                out_ref[0, r, :, c * ATTN_W:(c + 1) * ATTN_W] = res[r * rows:(r + 1) * rows]

    za = mm(hb, _ZA, ATTN_W)
    za_ref[0] = (za * _sigmoid(za)).astype(BF16)
    u_ref[0] = (mm(hb, _GC, CONV_W) * mm(hb, _HC, CONV_W)).astype(BF16)
    zb = mm(hb, _ZB, CONV_W)
    pre_ref[0] = (zb * _sigmoid(zb) * mm(hb, _GB, CONV_W)).astype(BF16)


def _projection(x, norm_g, w_bf):
    b, s, _ = x.shape
    t = T_PROJ
    grid = (b, s // t)
    tok = lambda width: pl.BlockSpec((1, t, width), lambda bi, i: (bi, i, 0))
    res = lambda d: pl.BlockSpec((1, d, t // d, QKV_W), lambda bi, i: (bi, 0, i, 0))
    const = lambda shape: pl.BlockSpec(shape, lambda bi, i: (0,) * len(shape))
    out_shape = (
        jax.ShapeDtypeStruct((b, 1, s, QKV_W), BF16),
        jax.ShapeDtypeStruct((b, 4, s // 4, QKV_W), BF16),
        jax.ShapeDtypeStruct((b, 16, s // 16, QKV_W), BF16),
        jax.ShapeDtypeStruct((b, s, ATTN_W), BF16),
        jax.ShapeDtypeStruct((b, s, CONV_W), BF16),
        jax.ShapeDtypeStruct((b, s, CONV_W), BF16),
        jax.ShapeDtypeStruct((b, s, D_MODEL), BF16),
    )
    return pl.pallas_call(
        _proj_kernel,
        grid=grid,
        in_specs=[
            tok(D_MODEL),
            const((1, D_MODEL)),
            pl.BlockSpec((D_MODEL, _G), lambda bi, i: (0, 0), pipeline_mode=pl.Buffered(1)),
        ],
        out_specs=(res(1), res(4), res(16), tok(ATTN_W), tok(CONV_W), tok(CONV_W), tok(D_MODEL)),
        out_shape=out_shape,
        scratch_shapes=[pltpu.VMEM((N_SLAB, t, LANES), F32)],
        compiler_params=pltpu.CompilerParams(
            dimension_semantics=("arbitrary", "arbitrary"), vmem_limit_bytes=VMEM_LIMIT),
        name="proj",
    )(x, norm_g.reshape(1, D_MODEL), w_bf)


def _select(cond, a, b):
    if isinstance(cond, (bool, np.bool_)):
        return a() if cond else b()
    return jnp.where(cond, a(), b())


class _Group:
    def __init__(self, gi, cur, prev, nxt, bias_ref, tt, first, last, minor=1):
        self.d = DILATIONS[gi]
        self.minor = minor
        self.cur, self.prev, self.nxt, self.bias_ref = cur, prev, nxt, bias_ref
        self.tq = bias_ref.shape[1] // 2
        self.tk = bias_ref.shape[2]
        self.rows = tt // self.d
        self.bpr = self.rows // self.tq
        self.nsets = self.d * self.bpr
        self.first, self.last = first, last
        assert self.tk == self.tq + 2 * RADIUS and self.rows % self.tq == 0 and self.tq % RADIUS == 0

    def _split(self, n):
        if self.minor != 1:
            assert self.bpr == 1
            per = self.d // self.minor
            if isinstance(n, int):
                return self.minor * (n % per) + n // per, 0
            shift = per.bit_length() - 1
            assert per == 1 << shift
            return self.minor * (n & (per - 1)) + (n >> shift), 0
        if isinstance(n, int):
            return n // self.bpr, n % self.bpr
        if self.bpr == 1:
            return n, 0
        shift = self.bpr.bit_length() - 1
        assert self.bpr == 1 << shift
        return n >> shift, n & (self.bpr - 1)

    def _row(self, start, mult):
        return start if isinstance(start, int) else pl.multiple_of(start, mult)

    def q(self, n):
        r, blk = self._split(n)
        return self.cur[0, r, pl.ds(self._row(blk * self.tq, self.tq), self.tq), 2 * ATTN_W:QKV_W]

    def window(self, n, cols):
        r, blk = self._split(n)
        tq = self.tq
        at_lo = blk == 0
        at_hi = blk == self.bpr - 1
        if isinstance(blk, int):
            lo_start, hi_start = max(blk * tq - RADIUS, 0), min(blk * tq + tq, self.rows - RADIUS)
        else:
            lo_start = jnp.maximum(blk * tq - RADIUS, 0)
            hi_start = jnp.minimum(blk * tq + tq, self.rows - RADIUS)
        head = _select(at_lo, lambda: self.prev[0, r, :, cols],
                       lambda: self.cur[0, r, pl.ds(self._row(lo_start, RADIUS), RADIUS), cols])
        body = self.cur[0, r, pl.ds(self._row(blk * tq, tq), tq), cols]
        tail = _select(at_hi, lambda: self.nxt[0, r, :, cols],
                       lambda: self.cur[0, r, pl.ds(self._row(hi_start, RADIUS), RADIUS), cols])
        return jnp.concatenate([head, body, tail], axis=0)

    def col_ok(self, n):
        _, blk = self._split(n)
        col = lax.broadcasted_iota(jnp.int32, (1, self.tk), 1)
        lo_out = jnp.logical_and(self.first, blk == 0)
        hi_out = jnp.logical_and(self.last, blk == self.bpr - 1)
        bad = jnp.logical_or(jnp.logical_and(col < RADIUS, lo_out),
                             jnp.logical_and(col >= self.tk - RADIUS, hi_out))
        return jnp.logical_not(bad)

    def token_rows(self, n):
        r, blk = self._split(n)
        if self.d == 1:
            return pl.ds(self._row(blk * self.tq, self.tq), self.tq)
        return pl.ds(r + self.d * blk * self.tq, self.tq, stride=self.d)


def _bias_table(bias_ref, gi):
    slopes = _alibi_slopes()
    tq, tk = bias_ref.shape[1] // 2, bias_ref.shape[2]
    qi = lax.broadcasted_iota(jnp.int32, (tq, tk), 0)
    kj = lax.broadcasted_iota(jnp.int32, (tq, tk), 1)
    rel = jnp.abs(kj - RADIUS - qi)
    dist = (rel * DILATIONS[gi]).astype(F32)
    for h in range(HEADS):
        rows = slice((h % 2) * tq, (h % 2 + 1) * tq)
        bias_ref[h // 2, rows, :] = jnp.where(rel <= RADIUS, (-LOG2E * float(slopes[gi, h])) * dist, NEG_BIG)


def _make_stages(s_scr, p_scr, st_scr):
    low = lax.broadcasted_iota(jnp.int32, (1, LANES), 1) < HEAD_DIM
    zero = jnp.zeros((), BF16)

    def pick(a, tq):
        return jnp.where(low, jnp.broadcast_to(a[:tq], (tq, LANES)), jnp.broadcast_to(a[tq:], (tq, LANES)))

    def scores(g, n, slot):
        q, k = g.q(n), g.window(n, slice(0, ATTN_W))
        for hp in range(N_PAIR):
            sl = slice(hp * LANES, (hp + 1) * LANES)
            qp = q[:, sl]
            q2 = jnp.concatenate([jnp.where(low, qp, zero), jnp.where(low, zero, qp)], axis=0)
            s_scr[slot, hp, 0:2 * g.tq, 0:g.tk] = lax.dot_general(
                q2, k[:, sl], (((1,), (1,)), ((), ())), preferred_element_type=F32)

    def softmax(g, n, slot):
        ok = g.col_ok(n)
        for hp in range(N_PAIR):
            s = s_scr[slot, hp, 0:2 * g.tq, 0:g.tk] + g.bias_ref[hp]
            s = jnp.where(ok, s, NEG_BIG)
            m = jnp.max(s, axis=-1, keepdims=True)
            p = jnp.exp2(s - m)
            den = jnp.sum(p, axis=-1, keepdims=True)
            p_scr[slot, hp, 0:2 * g.tq, 0:g.tk] = p.astype(BF16)
            st_scr[slot, 0, hp, 0:g.tq] = pick(m, g.tq)
            st_scr[slot, 1, hp, 0:g.tq] = pick(den, g.tq)

    def values(g, n, slot):
        v = g.window(n, slice(ATTN_W, 2 * ATTN_W))
        for hp in range(N_PAIR):
            o = jnp.dot(p_scr[slot, hp, 0:2 * g.tq, 0:g.tk], v[:, hp * LANES:(hp + 1) * LANES],
                        preferred_element_type=F32)
            yield hp, pick(o, g.tq), st_scr[slot, 0, hp, 0:g.tq], st_scr[slot, 1, hp, 0:g.tq]

    return scores, softmax, values


def _attn_group_kernel(cur_ref, prev_ref, nxt_ref, o_ref, l_ref, bias_ref, s_scr, p_scr, st_scr, *rest, gi):
    bi, ti = pl.program_id(0), pl.program_id(1)
    first, last = ti == 0, ti == pl.num_programs(1) - 1
    tt = o_ref.shape[2]
    d = DILATIONS[gi]
    unroll = 8
    hop = 4
    two_hop = d == hop * hop
    (stage_ref,) = rest if two_hop else (None,)

    @pl.when(jnp.logical_and(bi == 0, ti == 0))
    def _init_bias():
        _bias_table(bias_ref, gi)

    g = _Group(gi, cur_ref, prev_ref, nxt_ref, bias_ref, tt, first, last, minor=hop if two_hop else 1)
    scores, softmax, values = _make_stages(s_scr, p_scr, st_scr)

    def finish(n, phase):
        slot = phase % 2
        if not two_hop:
            rows = g.token_rows(n)
            for hp, o, m, den in values(g, n, slot):
                o_ref[0, hp, rows, :] = o * (1.0 / den)
                l_ref[0, hp, rows, :] = m + jnp.log2(den)
            return
        chunk = n // hop if isinstance(n, int) else n >> 2
        buf = chunk % 2 if isinstance(chunk, int) else chunk & 1
        rows = pl.ds(phase % hop, g.tq, stride=hop)
        for hp, o, m, den in values(g, n, slot):
            stage_ref[buf, 0, hp, rows, :] = o * (1.0 / den)
            stage_ref[buf, 1, hp, rows, :] = m + jnp.log2(den)
        if phase % hop == hop - 1:
            spread = pl.ds(chunk, hop * g.tq, stride=hop)
            for hp in range(N_PAIR):
                o_ref[0, hp, spread, :] = stage_ref[buf, 0, hp]
                l_ref[0, hp, spread, :] = stage_ref[buf, 1, hp]

    def step(n, phase):
        static = isinstance(n, int)
        if not static or n >= 2:
            finish(n - 2, (phase - 2) % unroll)
        if not static or 1 <= n <= g.nsets:
            softmax(g, n - 1, (phase - 1) % 2)
        if not static or n < g.nsets:
            scores(g, n, phase % 2)

    assert g.nsets >= 2 * unroll and g.nsets % unroll == 0 and (not two_hop or g.tq * d == tt)
    for n in range(unroll):
        step(n, n)

    def steady(k, carry):
        for phase in range(unroll):
            step(unroll * k + phase, phase)
        return carry
    lax.fori_loop(1, g.nsets // unroll, steady, 0)
    for n in range(g.nsets, g.nsets + 2):
        step(n, n % unroll)


def _halo_specs(d, rows_per_tile, seq_rows, width):
    per_tile = rows_per_tile // RADIUS
    n_halo = seq_rows // RADIUS
    return [
        pl.BlockSpec((1, d, RADIUS, width), lambda bi, i: (bi, 0, jnp.maximum(i * per_tile - 1, 0), 0)),
        pl.BlockSpec((1, d, RADIUS, width), lambda bi, i: (bi, 0, jnp.minimum((i + 1) * per_tile, n_halo - 1), 0)),
    ]


def _attention_group(kvq, gi):
    b, d, rows, _ = kvq.shape
    s = rows * d
    tt = T_ATTN
    tq = 2 * RADIUS
    out_spec = pl.BlockSpec((1, N_PAIR, tt, LANES), lambda bi, i: (bi, 0, i, 0))
    out_sds = jax.ShapeDtypeStruct((b, N_PAIR, s, LANES), F32)
    return pl.pallas_call(
        functools.partial(_attn_group_kernel, gi=gi),
        grid=(b, s // tt),
        in_specs=[pl.BlockSpec((1, d, tt // d, QKV_W), lambda bi, i: (bi, 0, i, 0)),
                  *_halo_specs(d, tt // d, rows, 2 * ATTN_W)],
        out_specs=(out_spec, out_spec),
        out_shape=(out_sds, out_sds),
        scratch_shapes=[
            pltpu.VMEM((N_PAIR, 2 * tq, tq + 2 * RADIUS), F32),
            pltpu.VMEM((2, N_PAIR, 2 * tq, tq + 2 * RADIUS), F32),
            pltpu.VMEM((2, N_PAIR, 2 * tq, tq + 2 * RADIUS), BF16),
            pltpu.VMEM((2, 2, N_PAIR, tq, LANES), F32),
        ] + ([pltpu.VMEM((2, 2, N_PAIR, 4 * tq, LANES), F32)] if d == 16 else []),
        compiler_params=pltpu.CompilerParams(
            dimension_semantics=("arbitrary", "arbitrary"), vmem_limit_bytes=VMEM_LIMIT),
        name=f"attn{gi}",
    )(kvq, kvq, kvq)


def _out_kernel(x_ref, c0_ref, p0_ref, n0_ref, o1_ref, l1_ref, o2_ref, l2_ref,
                za_ref, u_ref, up_ref, un_ref, pre_ref, hb_ref,
                cw_ref, wa_ref, wc_ref, wga_ref, wgc_ref, bg_ref, wo_ref, fg_ref, y_ref,
                ubuf_ref, b0_ref, s_scr, p_scr, st_scr, attn_scr):
    bi, ti = pl.program_id(0), pl.program_id(1)
    first, last = ti == 0, ti == pl.num_programs(1) - 1
    t = x_ref.shape[1]
    pad = ubuf_ref.shape[0] - t
    lo = pad // 2

    @pl.when(jnp.logical_and(bi == 0, ti == 0))
    def _init_bias():
        _bias_table(b0_ref, 0)

    g0 = _Group(0, c0_ref, p0_ref, n0_ref, b0_ref, t, first, last)
    scores, softmax, values = _make_stages(s_scr, p_scr, st_scr)

    def finish(n, slot):
        rows = g0.token_rows(n)
        for hp, o, m, d in values(g0, n, slot):
            l1, l2 = l1_ref[0, hp, rows, :], l2_ref[0, hp, rows, :]
            top = jnp.maximum(jnp.maximum(m, l1), l2)
            e0, e1, e2 = jnp.exp2(m - top), jnp.exp2(l1 - top), jnp.exp2(l2 - top)
            num = e0 * o + e1 * o1_ref[0, hp, rows, :] + e2 * o2_ref[0, hp, rows, :]
            attn_scr[rows, hp * LANES:(hp + 1) * LANES] = num * (1.0 / (e0 * d + e1 + e2))

    hb = hb_ref[0]
    half = D_MODEL // 2

    def gate_piece(w_ref, j):
        cols = slice(j * half, (j + 1) * half)
        return jnp.dot(hb, w_ref[:, cols], preferred_element_type=F32)

    def conv_branch():
        ubuf_ref[0:lo, :] = jnp.where(first, 0.0, up_ref[0].astype(F32))
        ubuf_ref[lo:lo + t, :] = u_ref[0].astype(F32)
        ubuf_ref[lo + t:, :] = jnp.where(last, 0.0, un_ref[0].astype(F32))
        conv = (cw_ref[0:1, :] * ubuf_ref[lo - 1:lo - 1 + t, :]
                + cw_ref[1:2, :] * ubuf_ref[lo:lo + t, :]
                + cw_ref[2:3, :] * ubuf_ref[lo + 1:lo + 1 + t, :])
        c_in = (pre_ref[0].astype(F32) * conv).astype(BF16)
        return jnp.dot(c_in, wc_ref[...], preferred_element_type=F32)

    fillers = [lambda: gate_piece(wga_ref, 0), lambda: gate_piece(wga_ref, 1),
               lambda: gate_piece(wgc_ref, 0), lambda: gate_piece(wgc_ref, 1), conv_branch]
    filled = []
    assert g0.nsets + 2 >= len(fillers)
    for n in range(g0.nsets + 2):
        if n >= 2:
            finish(n - 2, n % 2)
        if 1 <= n <= g0.nsets:
            softmax(g0, n - 1, 1 - n % 2)
        if n < g0.nsets:
            scores(g0, n, n % 2)
        if n < len(fillers):
            filled.append(fillers[n]())
    ga0, ga1, gc0, gc1, c = filled
    gate_a = _sigmoid(jnp.concatenate([ga0, ga1], axis=1) + bg_ref[:, 0:D_MODEL])
    gate_c = _sigmoid(jnp.concatenate([gc0, gc1], axis=1) + bg_ref[:, D_MODEL:])

    a_in = (za_ref[0].astype(F32) * attn_scr[...]).astype(BF16)
    a = jnp.dot(a_in, wa_ref[...], preferred_element_type=F32)
    merged = gate_a * a + gate_c * c
    h = x_ref[0] + jnp.dot(merged.astype(BF16), wo_ref[...], preferred_element_type=F32)
    ms = jnp.mean(h * h, axis=-1, keepdims=True)
    y_ref[0] = h * lax.rsqrt(ms + RMS_EPS) * fg_ref[...]


def _output(x, kvq0, groups12, za, u, pre, hb, conv_w, w_bf, b_gate, wa_bf, wc_bf, wo_bf, final_g):
    b, s, _ = x.shape
    t = T_OUT
    halo = 8
    tq0 = 2 * RADIUS
    tok = lambda width: pl.BlockSpec((1, t, width), lambda bi, i: (bi, i, 0))
    slabs = pl.BlockSpec((1, N_PAIR, t, LANES), lambda bi, i: (bi, 0, i, 0))
    const = lambda shape: pl.BlockSpec(shape, lambda bi, i: (0,) * len(shape), pipeline_mode=pl.Buffered(1))
    gate_w = lambda j: pl.BlockSpec((D_MODEL, D_MODEL), lambda bi, i: (0, _G // D_MODEL + j),
                                    pipeline_mode=pl.Buffered(1))
    assert _G % D_MODEL == 0
    per_tile = t // halo
    n_halo = s // halo
    return pl.pallas_call(
        _out_kernel,
        grid=(b, s // t),
        in_specs=[
            tok(D_MODEL),
            pl.BlockSpec((1, 1, t, QKV_W), lambda bi, i: (bi, 0, i, 0)), *_halo_specs(1, t, s, 2 * ATTN_W),
            slabs, slabs, slabs, slabs,
            tok(ATTN_W), tok(CONV_W),
            pl.BlockSpec((1, halo, CONV_W), lambda bi, i: (bi, jnp.maximum(i * per_tile - 1, 0), 0)),
            pl.BlockSpec((1, halo, CONV_W), lambda bi, i: (bi, jnp.minimum((i + 1) * per_tile, n_halo - 1), 0)),
            tok(CONV_W), tok(D_MODEL),
            const((3, CONV_W)), const((ATTN_W, D_MODEL)), const((CONV_W, D_MODEL)),
            gate_w(0), gate_w(1), const((1, 2 * D_MODEL)),
            const((D_MODEL, D_MODEL)), const((1, D_MODEL)),
        ],
        out_specs=tok(D_MODEL),
        out_shape=jax.ShapeDtypeStruct((b, s, D_MODEL), F32),
        scratch_shapes=[
            pltpu.VMEM((t + 2 * halo, CONV_W), F32),
            pltpu.VMEM((N_PAIR, 2 * tq0, tq0 + 2 * RADIUS), F32),
            pltpu.VMEM((2, N_PAIR, 2 * tq0, tq0 + 2 * RADIUS), F32),
            pltpu.VMEM((2, N_PAIR, 2 * tq0, tq0 + 2 * RADIUS), BF16),
            pltpu.VMEM((2, 2, N_PAIR, tq0, LANES), F32),
            pltpu.VMEM((t, ATTN_W), F32),
        ],
        compiler_params=pltpu.CompilerParams(
            dimension_semantics=("arbitrary", "arbitrary"), vmem_limit_bytes=VMEM_LIMIT),
        name="out",
    )(x, kvq0, kvq0, kvq0, *groups12, za, u, u, u, pre, hb, conv_w, wa_bf, wc_bf, w_bf, w_bf,
      b_gate.reshape(1, 2 * D_MODEL), wo_bf, final_g.reshape(1, D_MODEL))


def _layer_and_norm(x, norm_g, w_bf, b_gate, conv_w, wa_bf, wc_bf, wo_bf, final_g):
    kvq0, kvq1, kvq2, za, u, pre, hb = _projection(x, norm_g, w_bf)
    groups12 = (*_attention_group(kvq1, 1), *_attention_group(kvq2, 2))
    return _output(x, kvq0, groups12, za, u, pre, hb, conv_w, w_bf, b_gate, wa_bf, wc_bf, wo_bf, final_g)


def kernel(x_prompt, x_sample, norm_g, w_in, b_gate, conv_w, w_attn_out, w_conv_out, w_o, final_g):
    assert norm_g.shape[0] == 1, "single layer"
    args = (norm_g[0], w_in[0].astype(BF16), b_gate[0], conv_w[0], w_attn_out[0].astype(BF16),
            w_conv_out[0].astype(BF16), w_o[0].astype(BF16), final_g)
    return (_layer_and_norm(x_prompt, *args), _layer_and_norm(x_sample, *args))
```

```python
import functools
import math

import jax
import jax.numpy as jnp
import numpy as np
from jax import lax
from jax.experimental import pallas as pl
from jax.experimental.pallas import tpu as pltpu

F32 = jnp.float32
BF16 = jnp.bfloat16

D_MODEL = 1024
N_GROUPS = 3
DILATIONS = (1, 4, 16)
WINDOWS = (128, 512, 2048)
HEADS = 8
HEAD_DIM = 64
ATTN_W = HEADS * HEAD_DIM
QKV_W = N_GROUPS * ATTN_W
CONV_W = 512
IN_W = 3 * QKV_W + ATTN_W + 4 * CONV_W + 2 * D_MODEL
RADIUS = 64
RMS_EPS = 1e-6
NEG_BIG = -1e30
LOG2E = math.log2(math.e)
Q_SCALE = LOG2E / math.sqrt(HEAD_DIM)

LANES = 128
N_SLAB = D_MODEL // LANES
N_PAIR = ATTN_W // LANES

_Q0, _K0, _V0 = 0, QKV_W, 2 * QKV_W
_ZA = 3 * QKV_W
_HC = _ZA + ATTN_W
_GB = _HC + CONV_W
_GC = _GB + CONV_W
_ZB = _GC + CONV_W
_G = _ZB + CONV_W

T_PROJ = 512
T_ATTN = 2048
T_OUT = 512
VMEM_LIMIT = 56 * 1024 * 1024


def _alibi_slopes():
    n = N_GROUPS * HEADS
    s = 2.0 ** (-8.0 * np.arange(1, n + 1) / n)
    return s.astype(np.float32).reshape(N_GROUPS, HEADS)


def _sigmoid(z):
    return 1.0 / (1.0 + jnp.exp(-z))


def _proj_kernel(x_ref, g_ref, w_ref,
                 kvq0_ref, kvq1_ref, kvq2_ref, za_ref, u_ref, pre_ref, hb_ref,
                 hslab_ref):
    t = x_ref.shape[1]
    x = x_ref[0]
    ms = jnp.mean(x * x, axis=-1, keepdims=True)
    hn = x * lax.rsqrt(ms + RMS_EPS) * g_ref[...]
    hb = hn.astype(BF16)
    hb_ref[0] = hb

    def mm(h, c0, width):
        return jnp.dot(h, w_ref[:, c0:c0 + width], preferred_element_type=F32)

    kvq0_ref[0, 0, :, 0:ATTN_W] = mm(hb, _K0, ATTN_W).astype(BF16)
    kvq0_ref[0, 0, :, ATTN_W:2 * ATTN_W] = mm(hb, _V0, ATTN_W).astype(BF16)
    kvq0_ref[0, 0, :, 2 * ATTN_W:] = (mm(hb, _Q0, ATTN_W) * Q_SCALE).astype(BF16)

    for s in range(N_SLAB):
        hslab_ref[s] = hn[:, s * LANES:(s + 1) * LANES]
    for gi, out_ref in ((1, kvq1_ref), (2, kvq2_ref)):
        d = DILATIONS[gi]
        rows = t // d
        parts = []
        for r in range(d):
            parts.append(jnp.concatenate(
                [hslab_ref[s, pl.ds(r, rows, stride=d), :] for s in range(N_SLAB)], axis=-1))
        hr = jnp.concatenate(parts, axis=0).astype(BF16)
        for c, (col, scale) in enumerate(((_K0, None), (_V0, None), (_Q0, Q_SCALE))):
            res = mm(hr, col + gi * ATTN_W, ATTN_W)
            if scale is not None:
                res = res * scale
            res = res.astype(BF16)
            for r in range(d):
                out_ref[0, r, :, c * ATTN_W:(c + 1) * ATTN_W] = res[r * rows:(r + 1) * rows]

    za = mm(hb, _ZA, ATTN_W)
    za_ref[0] = (za * _sigmoid(za)).astype(BF16)
    u_ref[0] = (mm(hb, _GC, CONV_W) * mm(hb, _HC, CONV_W)).astype(BF16)
    zb = mm(hb, _ZB, CONV_W)
    pre_ref[0] = (zb * _sigmoid(zb) * mm(hb, _GB, CONV_W)).astype(BF16)


def _projection(x, norm_g, w_bf):
    b, s, _ = x.shape
    t = T_PROJ
    grid = (b, s // t)
    tok = lambda width: pl.BlockSpec((1, t, width), lambda bi, i: (bi, i, 0))
    res = lambda d: pl.BlockSpec((1, d, t // d, QKV_W), lambda bi, i: (bi, 0, i, 0))
    const = lambda shape: pl.BlockSpec(shape, lambda bi, i: (0,) * len(shape))
    out_shape = (
        jax.ShapeDtypeStruct((b, 1, s, QKV_W), BF16),
        jax.ShapeDtypeStruct((b, 4, s // 4, QKV_W), BF16),
        jax.ShapeDtypeStruct((b, 16, s // 16, QKV_W), BF16),
        jax.ShapeDtypeStruct((b, s, ATTN_W), BF16),
        jax.ShapeDtypeStruct((b, s, CONV_W), BF16),
        jax.ShapeDtypeStruct((b, s, CONV_W), BF16),
        jax.ShapeDtypeStruct((b, s, D_MODEL), BF16),
    )
    return pl.pallas_call(
        _proj_kernel,
        grid=grid,
        in_specs=[
            tok(D_MODEL),
            const((1, D_MODEL)),
            pl.BlockSpec((D_MODEL, _G), lambda bi, i: (0, 0), pipeline_mode=pl.Buffered(1)),
        ],
        out_specs=(res(1), res(4), res(16), tok(ATTN_W), tok(CONV_W), tok(CONV_W), tok(D_MODEL)),
        out_shape=out_shape,
        scratch_shapes=[pltpu.VMEM((N_SLAB, t, LANES), F32)],
        compiler_params=pltpu.CompilerParams(
            dimension_semantics=("arbitrary", "arbitrary"), vmem_limit_bytes=VMEM_LIMIT),
        name="proj",
    )(x, norm_g.reshape(1, D_MODEL), w_bf)


class _Group:
    def __init__(self, gi, cur, prev, nxt, bias_ref, tt, first, last, minor=1):
        self.d = DILATIONS[gi]
        self.minor = minor
        self.cur, self.prev, self.nxt, self.bias_ref = cur, prev, nxt, bias_ref
        self.tq = bias_ref.shape[1] // 2
        self.tk = bias_ref.shape[2]
        self.rows = tt // self.d
        self.bpr = self.rows // self.tq
        self.nsets = self.d * self.bpr
        self.first, self.last = first, last
        assert self.tk == self.tq + 2 * RADIUS and self.rows % self.tq == 0 and self.d % minor == 0
        assert minor == 1 or self.bpr == 1

    def _split(self, n):
        if self.minor != 1:
            per = self.d // self.minor
            return self.minor * (n % per) + n // per, 0
        return n // self.bpr, n % self.bpr

    def q(self, n):
        r, blk = self._split(n)
        return self.cur[0, r, blk * self.tq:(blk + 1) * self.tq, 2 * ATTN_W:QKV_W]

    def window(self, n, cols):
        r, blk = self._split(n)
        lo, hi = blk * self.tq - RADIUS, (blk + 1) * self.tq + RADIUS
        parts = [self.cur[0, r, max(lo, 0):min(hi, self.rows), cols]]
        if lo < 0:
            parts.insert(0, self.prev[0, r, :, cols])
        if hi > self.rows:
            parts.append(self.nxt[0, r, :, cols])
        return jnp.concatenate(parts, axis=0) if len(parts) > 1 else parts[0]

    def col_ok(self, n):
        _, blk = self._split(n)
        col = lax.broadcasted_iota(jnp.int32, (1, self.tk), 1)
        bad = []
        if blk == 0:
            bad.append(jnp.logical_and(col < RADIUS, self.first))
        if blk == self.bpr - 1:
            bad.append(jnp.logical_and(col >= self.tk - RADIUS, self.last))
        if not bad:
            return None
        return jnp.logical_not(functools.reduce(jnp.logical_or, bad))

    def token_rows(self, n):
        r, blk = self._split(n)
        if self.d == 1:
            return pl.ds(blk * self.tq, self.tq)
        return pl.ds(r + self.d * blk * self.tq, self.tq, stride=self.d)


def _bias_table(bias_ref, gi):
    slopes = _alibi_slopes()
    tq, tk = bias_ref.shape[1] // 2, bias_ref.shape[2]
    qi = lax.broadcasted_iota(jnp.int32, (tq, tk), 0)
    kj = lax.broadcasted_iota(jnp.int32, (tq, tk), 1)
    rel = jnp.abs(kj - RADIUS - qi)
    dist = (rel * DILATIONS[gi]).astype(F32)
    for h in range(HEADS):
        rows = slice((h % 2) * tq, (h % 2 + 1) * tq)
        bias_ref[h // 2, rows, :] = jnp.where(rel <= RADIUS, (-LOG2E * float(slopes[gi, h])) * dist, NEG_BIG)


def _make_stages(s_scr, p_scr, st_scr):
    low = lax.broadcasted_iota(jnp.int32, (1, LANES), 1) < HEAD_DIM
    zero = jnp.zeros((), BF16)

    def pick(a, tq):
        return jnp.where(low, jnp.broadcast_to(a[:tq], (tq, LANES)), jnp.broadcast_to(a[tq:], (tq, LANES)))

    def scores(g, n, slot):
        q, k = g.q(n), g.window(n, slice(0, ATTN_W))
        for hp in range(N_PAIR):
            sl = slice(hp * LANES, (hp + 1) * LANES)
            qp = q[:, sl]
            q2 = jnp.concatenate([jnp.where(low, qp, zero), jnp.where(low, zero, qp)], axis=0)
            s_scr[slot, hp, 0:2 * g.tq, 0:g.tk] = lax.dot_general(
                q2, k[:, sl], (((1,), (1,)), ((), ())), preferred_element_type=F32)

    def softmax(g, n, slot):
        ok = g.col_ok(n)
        for hp in range(N_PAIR):
            s = s_scr[slot, hp, 0:2 * g.tq, 0:g.tk] + g.bias_ref[hp]
            if ok is not None:
                s = jnp.where(ok, s, NEG_BIG)
            m = jnp.max(s, axis=-1, keepdims=True)
            p = jnp.exp2(s - m)
            den = jnp.sum(p, axis=-1, keepdims=True)
            p_scr[slot, hp, 0:2 * g.tq, 0:g.tk] = p.astype(BF16)
            st_scr[slot, 0, hp, 0:g.tq] = pick(m, g.tq)
            st_scr[slot, 1, hp, 0:g.tq] = pick(den, g.tq)

    def values(g, n, slot):
        v = g.window(n, slice(ATTN_W, 2 * ATTN_W))
        for hp in range(N_PAIR):
            o = jnp.dot(p_scr[slot, hp, 0:2 * g.tq, 0:g.tk], v[:, hp * LANES:(hp + 1) * LANES],
                        preferred_element_type=F32)
            yield hp, pick(o, g.tq), st_scr[slot, 0, hp, 0:g.tq], st_scr[slot, 1, hp, 0:g.tq]

    return scores, softmax, values


def _attn_group_kernel(cur_ref, prev_ref, nxt_ref, o_ref, l_ref, bias_ref, s_scr, p_scr, st_scr, *rest, gi):
    bi, ti = pl.program_id(0), pl.program_id(1)
    first, last = ti == 0, ti == pl.num_programs(1) - 1
    tt = o_ref.shape[2]
    d = DILATIONS[gi]
    hop = 4
    two_hop = d == hop * hop
    (stage_ref,) = rest if two_hop else (None,)

    @pl.when(jnp.logical_and(bi == 0, ti == 0))
    def _init_bias():
        _bias_table(bias_ref, gi)

    g = _Group(gi, cur_ref, prev_ref, nxt_ref, bias_ref, tt, first, last, minor=hop if two_hop else 1)
    scores, softmax, values = _make_stages(s_scr, p_scr, st_scr)

    def finish(n):
        slot = n % 2
        if not two_hop:
            rows = g.token_rows(n)
            for hp, o, m, den in values(g, n, slot):
                o_ref[0, hp, rows, :] = o * (1.0 / den)
                l_ref[0, hp, rows, :] = m + jnp.log2(den)
            return
        chunk, a = n // hop, n % hop
        buf = chunk % 2
        rows = pl.ds(a, g.tq, stride=hop)
        for hp, o, m, den in values(g, n, slot):
            stage_ref[buf, 0, hp, rows, :] = o * (1.0 / den)
            stage_ref[buf, 1, hp, rows, :] = m + jnp.log2(den)
        if a == hop - 1:
            spread = pl.ds(chunk, hop * g.tq, stride=hop)
            for hp in range(N_PAIR):
                o_ref[0, hp, spread, :] = stage_ref[buf, 0, hp]
                l_ref[0, hp, spread, :] = stage_ref[buf, 1, hp]

    assert g.nsets % hop == 0 and (not two_hop or g.tq * d == tt)
    for n in range(g.nsets + 2):
        if n >= 2:
            finish(n - 2)
        if 1 <= n <= g.nsets:
            softmax(g, n - 1, (n - 1) % 2)
        if n < g.nsets:
            scores(g, n, n % 2)


def _halo_specs(d, rows_per_tile, seq_rows, width):
    per_tile = rows_per_tile // RADIUS
    n_halo = seq_rows // RADIUS
    return [
        pl.BlockSpec((1, d, RADIUS, width), lambda bi, i: (bi, 0, jnp.maximum(i * per_tile - 1, 0), 0)),
        pl.BlockSpec((1, d, RADIUS, width), lambda bi, i: (bi, 0, jnp.minimum((i + 1) * per_tile, n_halo - 1), 0)),
    ]


def _attention_group(kvq, gi):
    b, d, rows, _ = kvq.shape
    s = rows * d
    tt = T_ATTN
    tq = 2 * RADIUS
    out_spec = pl.BlockSpec((1, N_PAIR, tt, LANES), lambda bi, i: (bi, 0, i, 0))
    out_sds = jax.ShapeDtypeStruct((b, N_PAIR, s, LANES), F32)
    return pl.pallas_call(
        functools.partial(_attn_group_kernel, gi=gi),
        grid=(b, s // tt),
        in_specs=[pl.BlockSpec((1, d, tt // d, QKV_W), lambda bi, i: (bi, 0, i, 0)),
                  *_halo_specs(d, tt // d, rows, 2 * ATTN_W)],
        out_specs=(out_spec, out_spec),
        out_shape=(out_sds, out_sds),
        scratch_shapes=[
            pltpu.VMEM((N_PAIR, 2 * tq, tq + 2 * RADIUS), F32),
            pltpu.VMEM((2, N_PAIR, 2 * tq, tq + 2 * RADIUS), F32),
            pltpu.VMEM((2, N_PAIR, 2 * tq, tq + 2 * RADIUS), BF16),
            pltpu.VMEM((2, 2, N_PAIR, tq, LANES), F32),
        ] + ([pltpu.VMEM((2, 2, N_PAIR, 4 * tq, LANES), F32)] if d == 16 else []),
        compiler_params=pltpu.CompilerParams(
            dimension_semantics=("arbitrary", "arbitrary"), vmem_limit_bytes=VMEM_LIMIT),
        name=f"attn{gi}",
    )(kvq, kvq, kvq)


def _out_kernel(x_ref, c0_ref, p0_ref, n0_ref, o1_ref, l1_ref, o2_ref, l2_ref,
                za_ref, u_ref, up_ref, un_ref, pre_ref, hb_ref,
                cw_ref, wa_ref, wc_ref, wga_ref, wgc_ref, bg_ref, wo_ref, fg_ref, y_ref,
                ubuf_ref, b0_ref, s_scr, p_scr, st_scr, attn_scr):
    bi, ti = pl.program_id(0), pl.program_id(1)
    first, last = ti == 0, ti == pl.num_programs(1) - 1
    t = x_ref.shape[1]
    pad = ubuf_ref.shape[0] - t
    lo = pad // 2

    @pl.when(jnp.logical_and(bi == 0, ti == 0))
    def _init_bias():
        _bias_table(b0_ref, 0)

    g0 = _Group(0, c0_ref, p0_ref, n0_ref, b0_ref, t, first, last)
    scores, softmax, values = _make_stages(s_scr, p_scr, st_scr)

    def finish(n, slot):
        rows = g0.token_rows(n)
        for hp, o, m, d in values(g0, n, slot):
            l1, l2 = l1_ref[0, hp, rows, :], l2_ref[0, hp, rows, :]
            top = jnp.maximum(jnp.maximum(m, l1), l2)
            e0, e1, e2 = jnp.exp2(m - top), jnp.exp2(l1 - top), jnp.exp2(l2 - top)
            num = e0 * o + e1 * o1_ref[0, hp, rows, :] + e2 * o2_ref[0, hp, rows, :]
            attn_scr[rows, hp * LANES:(hp + 1) * LANES] = num * (1.0 / (e0 * d + e1 + e2))

    hb = hb_ref[0]
    half = D_MODEL // 2

    def gate_piece(w_ref, j):
        cols = slice(j * half, (j + 1) * half)
        return jnp.dot(hb, w_ref[:, cols], preferred_element_type=F32)

    def conv_branch():
        ubuf_ref[0:lo, :] = jnp.where(first, 0.0, up_ref[0].astype(F32))
        ubuf_ref[lo:lo + t, :] = u_ref[0].astype(F32)
        ubuf_ref[lo + t:, :] = jnp.where(last, 0.0, un_ref[0].astype(F32))
        conv = (cw_ref[0:1, :] * ubuf_ref[lo - 1:lo - 1 + t, :]
                + cw_ref[1:2, :] * ubuf_ref[lo:lo + t, :]
                + cw_ref[2:3, :] * ubuf_ref[lo + 1:lo + 1 + t, :])
        c_in = (pre_ref[0].astype(F32) * conv).astype(BF16)
        return jnp.dot(c_in, wc_ref[...], preferred_element_type=F32)

    fillers = [conv_branch, lambda: gate_piece(wga_ref, 0), lambda: gate_piece(wga_ref, 1),
               lambda: gate_piece(wgc_ref, 0), lambda: gate_piece(wgc_ref, 1)]
    filled = []
    assert g0.nsets + 2 >= len(fillers)
    for n in range(g0.nsets + 2):
        if n >= 2:
            finish(n - 2, n % 2)
        if 1 <= n <= g0.nsets:
            softmax(g0, n - 1, 1 - n % 2)
        if n < g0.nsets:
            scores(g0, n, n % 2)
        if n < len(fillers):
            filled.append(fillers[n]())
    c, ga0, ga1, gc0, gc1 = filled
    gate_a = _sigmoid(jnp.concatenate([ga0, ga1], axis=1) + bg_ref[:, 0:D_MODEL])
    gate_c = _sigmoid(jnp.concatenate([gc0, gc1], axis=1) + bg_ref[:, D_MODEL:])

    a_in = (za_ref[0].astype(F32) * attn_scr[...]).astype(BF16)
    a = jnp.dot(a_in, wa_ref[...], preferred_element_type=F32)
    merged = gate_a * a + gate_c * c
    h = x_ref[0] + jnp.dot(merged.astype(BF16), wo_ref[...], preferred_element_type=F32)
    ms = jnp.mean(h * h, axis=-1, keepdims=True)
    y_ref[0] = h * lax.rsqrt(ms + RMS_EPS) * fg_ref[...]


def _output(x, kvq0, groups12, za, u, pre, hb, conv_w, w_bf, b_gate, wa_bf, wc_bf, wo_bf, final_g):
    b, s, _ = x.shape
    t = T_OUT
    halo = 8
    tq0 = 2 * RADIUS
    tok = lambda width: pl.BlockSpec((1, t, width), lambda bi, i: (bi, i, 0))
    slabs = pl.BlockSpec((1, N_PAIR, t, LANES), lambda bi, i: (bi, 0, i, 0))
    const = lambda shape: pl.BlockSpec(shape, lambda bi, i: (0,) * len(shape), pipeline_mode=pl.Buffered(1))
    gate_w = lambda j: pl.BlockSpec((D_MODEL, D_MODEL), lambda bi, i: (0, _G // D_MODEL + j),
                                    pipeline_mode=pl.Buffered(1))
    assert _G % D_MODEL == 0
    per_tile = t // halo
    n_halo = s // halo
    return pl.pallas_call(
        _out_kernel,
        grid=(b, s // t),
        in_specs=[
            tok(D_MODEL),
            pl.BlockSpec((1, 1, t, QKV_W), lambda bi, i: (bi, 0, i, 0)), *_halo_specs(1, t, s, 2 * ATTN_W),
            slabs, slabs, slabs, slabs,
            tok(ATTN_W), tok(CONV_W),
            pl.BlockSpec((1, halo, CONV_W), lambda bi, i: (bi, jnp.maximum(i * per_tile - 1, 0), 0)),
            pl.BlockSpec((1, halo, CONV_W), lambda bi, i: (bi, jnp.minimum((i + 1) * per_tile, n_halo - 1), 0)),
            tok(CONV_W), tok(D_MODEL),
            const((3, CONV_W)), const((ATTN_W, D_MODEL)), const((CONV_W, D_MODEL)),
            gate_w(0), gate_w(1), const((1, 2 * D_MODEL)),
            const((D_MODEL, D_MODEL)), const((1, D_MODEL)),
        ],
        out_specs=tok(D_MODEL),
        out_shape=jax.ShapeDtypeStruct((b, s, D_MODEL), F32),
        scratch_shapes=[
            pltpu.VMEM((t + 2 * halo, CONV_W), F32),
            pltpu.VMEM((N_PAIR, 2 * tq0, tq0 + 2 * RADIUS), F32),
            pltpu.VMEM((2, N_PAIR, 2 * tq0, tq0 + 2 * RADIUS), F32),
            pltpu.VMEM((2, N_PAIR, 2 * tq0, tq0 + 2 * RADIUS), BF16),
            pltpu.VMEM((2, 2, N_PAIR, tq0, LANES), F32),
            pltpu.VMEM((t, ATTN_W), F32),
        ],
        compiler_params=pltpu.CompilerParams(
            dimension_semantics=("arbitrary", "arbitrary"), vmem_limit_bytes=VMEM_LIMIT),
        name="out",
    )(x, kvq0, kvq0, kvq0, *groups12, za, u, u, u, pre, hb, conv_w, wa_bf, wc_bf, w_bf, w_bf,
      b_gate.reshape(1, 2 * D_MODEL), wo_bf, final_g.reshape(1, D_MODEL))


def _layer_and_norm(x, norm_g, w_bf, b_gate, conv_w, wa_bf, wc_bf, wo_bf, final_g):
    kvq0, kvq1, kvq2, za, u, pre, hb = _projection(x, norm_g, w_bf)
    groups12 = (*_attention_group(kvq1, 1), *_attention_group(kvq2, 2))
    return _output(x, kvq0, groups12, za, u, pre, hb, conv_w, w_bf, b_gate, wa_bf, wc_bf, wo_bf, final_g)


def kernel(x_prompt, x_sample, norm_g, w_in, b_gate, conv_w, w_attn_out, w_conv_out, w_o, final_g):
    assert norm_g.shape[0] == 1, "single layer"
    args = (norm_g[0], w_in[0].astype(BF16), b_gate[0], conv_w[0], w_attn_out[0].astype(BF16),
            w_conv_out[0].astype(BF16), w_o[0].astype(BF16), final_g)
    return (_layer_and_norm(x_prompt, *args), _layer_and_norm(x_sample, *args))
```

```python
import functools
import math

import jax
import jax.numpy as jnp
import numpy as np
from jax import lax
from jax.experimental import pallas as pl
from jax.experimental.pallas import tpu as pltpu

F32 = jnp.float32
BF16 = jnp.bfloat16

D_MODEL = 1024
N_GROUPS = 3
DILATIONS = (1, 4, 16)
WINDOWS = (128, 512, 2048)
HEADS = 8
HEAD_DIM = 64
ATTN_W = HEADS * HEAD_DIM
QKV_W = N_GROUPS * ATTN_W
CONV_W = 512
IN_W = 3 * QKV_W + ATTN_W + 4 * CONV_W + 2 * D_MODEL
RADIUS = 64
RMS_EPS = 1e-6
NEG_BIG = -1e30
LOG2E = math.log2(math.e)
Q_SCALE = LOG2E / math.sqrt(HEAD_DIM)

LANES = 128
N_SLAB = D_MODEL // LANES
N_PAIR = ATTN_W // LANES

_Q0, _K0, _V0 = 0, QKV_W, 2 * QKV_W
_ZA = 3 * QKV_W
_HC = _ZA + ATTN_W
_GB = _HC + CONV_W
_GC = _GB + CONV_W
_ZB = _GC + CONV_W
_G = _ZB + CONV_W

T_PROJ = 512
T_ATTN = 2048
T_OUT = 512
PLANES = 4
assert all(d == 1 or d % PLANES == 0 for d in DILATIONS) and T_PROJ == T_OUT
VMEM_LIMIT = 56 * 1024 * 1024


def _alibi_slopes():
    n = N_GROUPS * HEADS
    s = 2.0 ** (-8.0 * np.arange(1, n + 1) / n)
    return s.astype(np.float32).reshape(N_GROUPS, HEADS)


def _sigmoid(z):
    return 1.0 / (1.0 + jnp.exp(-z))


def _proj_kernel(x_ref, g_ref, w_ref,
                 kvq0_ref, kvq1_ref, kvq2_ref, za_ref, u_ref, pre_ref, hb_ref,
                 hslab_ref):
    t = x_ref.shape[1]
    x = x_ref[0]
    ms = jnp.mean(x * x, axis=-1, keepdims=True)
    hn = x * lax.rsqrt(ms + RMS_EPS) * g_ref[...]
    hb = hn.astype(BF16)
    hb_ref[0] = hb

    def mm(h, c0, width):
        return jnp.dot(h, w_ref[:, c0:c0 + width], preferred_element_type=F32)

    kvq0_ref[0, 0, :, 0:ATTN_W] = mm(hb, _K0, ATTN_W).astype(BF16)
    kvq0_ref[0, 0, :, ATTN_W:2 * ATTN_W] = mm(hb, _V0, ATTN_W).astype(BF16)

    for s in range(N_SLAB):
        hslab_ref[s] = hn[:, s * LANES:(s + 1) * LANES]
    for gi, out_ref in ((1, kvq1_ref), (2, kvq2_ref)):
        d = DILATIONS[gi]
        rows = t // d
        parts = []
        for r in range(d):
            parts.append(jnp.concatenate(
                [hslab_ref[s, pl.ds(r, rows, stride=d), :] for s in range(N_SLAB)], axis=-1))
        hr = jnp.concatenate(parts, axis=0).astype(BF16)
        if d == PLANES:
            kvq0_ref[0, 0, :, 2 * ATTN_W:] = (mm(hr, _Q0, ATTN_W) * Q_SCALE).astype(BF16)
        for c, (col, scale) in enumerate(((_K0, None), (_V0, None), (_Q0, Q_SCALE))):
            res = mm(hr, col + gi * ATTN_W, ATTN_W)
            if scale is not None:
                res = res * scale
            res = res.astype(BF16)
            for r in range(d):
                out_ref[0, r, :, c * ATTN_W:(c + 1) * ATTN_W] = res[r * rows:(r + 1) * rows]

    za = mm(hb, _ZA, ATTN_W)
    za_ref[0] = (za * _sigmoid(za)).astype(BF16)
    u_ref[0] = (mm(hb, _GC, CONV_W) * mm(hb, _HC, CONV_W)).astype(BF16)
    zb = mm(hb, _ZB, CONV_W)
    pre_ref[0] = (zb * _sigmoid(zb) * mm(hb, _GB, CONV_W)).astype(BF16)


def _projection(x, norm_g, w_bf):
    b, s, _ = x.shape
    t = T_PROJ
    grid = (b, s // t)
    tok = lambda width: pl.BlockSpec((1, t, width), lambda bi, i: (bi, i, 0))
    res = lambda d: pl.BlockSpec((1, d, t // d, QKV_W), lambda bi, i: (bi, 0, i, 0))
    const = lambda shape: pl.BlockSpec(shape, lambda bi, i: (0,) * len(shape))
    out_shape = (
        jax.ShapeDtypeStruct((b, 1, s, QKV_W), BF16),
        jax.ShapeDtypeStruct((b, 4, s // 4, QKV_W), BF16),
        jax.ShapeDtypeStruct((b, 16, s // 16, QKV_W), BF16),
        jax.ShapeDtypeStruct((b, s, ATTN_W), BF16),
        jax.ShapeDtypeStruct((b, s, CONV_W), BF16),
        jax.ShapeDtypeStruct((b, s, CONV_W), BF16),
        jax.ShapeDtypeStruct((b, s, D_MODEL), BF16),
    )
    return pl.pallas_call(
        _proj_kernel,
        grid=grid,
        in_specs=[
            tok(D_MODEL),
            const((1, D_MODEL)),
            pl.BlockSpec((D_MODEL, _G), lambda bi, i: (0, 0), pipeline_mode=pl.Buffered(1)),
        ],
        out_specs=(res(1), res(4), res(16), tok(ATTN_W), tok(CONV_W), tok(CONV_W), tok(D_MODEL)),
        out_shape=out_shape,
        scratch_shapes=[pltpu.VMEM((N_SLAB, t, LANES), F32)],
        compiler_params=pltpu.CompilerParams(
            dimension_semantics=("arbitrary", "arbitrary"), vmem_limit_bytes=VMEM_LIMIT),
        name="proj",
    )(x, norm_g.reshape(1, D_MODEL), w_bf)


class _Group:
    def __init__(self, gi, cur, prev, nxt, bias_ref, tt, first, last):
        self.d = DILATIONS[gi]
        self.cur, self.prev, self.nxt, self.bias_ref = cur, prev, nxt, bias_ref
        self.tq = bias_ref.shape[1] // 2
        self.tk = bias_ref.shape[2]
        self.rows = tt // self.d
        self.bpr = self.rows // self.tq
        self.nsets = self.d * self.bpr
        self.first, self.last = first, last
        assert self.tk == self.tq + 2 * RADIUS and self.rows % self.tq == 0 and self.tq % PLANES == 0

    def _split(self, n):
        return n // self.bpr, n % self.bpr

    def q(self, n):
        r, blk = self._split(n)
        cols = slice(2 * ATTN_W, QKV_W)
        if self.d > 1:
            return self.cur[0, r, blk * self.tq:(blk + 1) * self.tq, cols]
        per, size = self.tq // PLANES, self.rows // PLANES
        return jnp.concatenate(
            [self.cur[0, r, c * size + blk * per:c * size + (blk + 1) * per, cols] for c in range(PLANES)], axis=0)

    def window(self, n, cols):
        r, blk = self._split(n)
        lo, hi = blk * self.tq - RADIUS, (blk + 1) * self.tq + RADIUS
        parts = [self.cur[0, r, max(lo, 0):min(hi, self.rows), cols]]
        if lo < 0:
            parts.insert(0, self.prev[0, r, :, cols])
        if hi > self.rows:
            parts.append(self.nxt[0, r, :, cols])
        return jnp.concatenate(parts, axis=0) if len(parts) > 1 else parts[0]

    def col_ok(self, n):
        _, blk = self._split(n)
        col = lax.broadcasted_iota(jnp.int32, (1, self.tk), 1)
        bad = []
        if blk == 0:
            bad.append(jnp.logical_and(col < RADIUS, self.first))
        if blk == self.bpr - 1:
            bad.append(jnp.logical_and(col >= self.tk - RADIUS, self.last))
        if not bad:
            return None
        return jnp.logical_not(functools.reduce(jnp.logical_or, bad))

    def plane_rows(self, n):
        r, blk = self._split(n)
        step = self.d // PLANES
        start = r // PLANES + step * blk * self.tq
        return r % PLANES, (pl.ds(start, self.tq) if step == 1 else pl.ds(start, self.tq, stride=step))


def _bias_table(bias_ref, gi):
    slopes = _alibi_slopes()
    tq, tk = bias_ref.shape[1] // 2, bias_ref.shape[2]
    qi = lax.broadcasted_iota(jnp.int32, (tq, tk), 0)
    if DILATIONS[gi] == 1:
        per = tq // PLANES
        shift = per.bit_length() - 1
        assert per == 1 << shift
        qi = (qi & (per - 1)) * PLANES + (qi >> shift)
    kj = lax.broadcasted_iota(jnp.int32, (tq, tk), 1)
    rel = jnp.abs(kj - RADIUS - qi)
    dist = (rel * DILATIONS[gi]).astype(F32)
    for h in range(HEADS):
        rows = slice((h % 2) * tq, (h % 2 + 1) * tq)
        bias_ref[h // 2, rows, :] = jnp.where(rel <= RADIUS, (-LOG2E * float(slopes[gi, h])) * dist, NEG_BIG)


def _make_stages(s_scr, p_scr, st_scr):
    low = lax.broadcasted_iota(jnp.int32, (1, LANES), 1) < HEAD_DIM
    zero = jnp.zeros((), BF16)

    def pick(a, tq):
        return jnp.where(low, jnp.broadcast_to(a[:tq], (tq, LANES)), jnp.broadcast_to(a[tq:], (tq, LANES)))

    def scores(g, n, slot):
        q, k = g.q(n), g.window(n, slice(0, ATTN_W))
        for hp in range(N_PAIR):
            sl = slice(hp * LANES, (hp + 1) * LANES)
            qp = q[:, sl]
            q2 = jnp.concatenate([jnp.where(low, qp, zero), jnp.where(low, zero, qp)], axis=0)
            s_scr[slot, hp, 0:2 * g.tq, 0:g.tk] = lax.dot_general(
                q2, k[:, sl], (((1,), (1,)), ((), ())), preferred_element_type=F32)

    def softmax(g, n, slot):
        ok = g.col_ok(n)
        for hp in range(N_PAIR):
            s = s_scr[slot, hp, 0:2 * g.tq, 0:g.tk] + g.bias_ref[hp]
            if ok is not None:
                s = jnp.where(ok, s, NEG_BIG)
            m = jnp.max(s, axis=-1, keepdims=True)
            p = jnp.exp2(s - m)
            den = jnp.sum(p, axis=-1, keepdims=True)
            p_scr[slot, hp, 0:2 * g.tq, 0:g.tk] = p.astype(BF16)
            st_scr[slot, 0, hp, 0:g.tq] = pick(m, g.tq)
            st_scr[slot, 1, hp, 0:g.tq] = pick(den, g.tq)

    def values(g, n, slot):
        v = g.window(n, slice(ATTN_W, 2 * ATTN_W))
        for hp in range(N_PAIR):
            o = jnp.dot(p_scr[slot, hp, 0:2 * g.tq, 0:g.tk], v[:, hp * LANES:(hp + 1) * LANES],
                        preferred_element_type=F32)
            yield hp, pick(o, g.tq), st_scr[slot, 0, hp, 0:g.tq], st_scr[slot, 1, hp, 0:g.tq]

    return scores, softmax, values


def _attn_group_kernel(cur_ref, prev_ref, nxt_ref, o_ref, l_ref, bias_ref, s_scr, p_scr, st_scr, *, gi):
    bi, ti = pl.program_id(0), pl.program_id(1)
    first, last = ti == 0, ti == pl.num_programs(1) - 1
    tt = PLANES * o_ref.shape[3]

    @pl.when(jnp.logical_and(bi == 0, ti == 0))
    def _init_bias():
        _bias_table(bias_ref, gi)

    g = _Group(gi, cur_ref, prev_ref, nxt_ref, bias_ref, tt, first, last)
    scores, softmax, values = _make_stages(s_scr, p_scr, st_scr)

    def finish(n):
        plane, rows = g.plane_rows(n)
        for hp, o, m, den in values(g, n, n % 2):
            o_ref[0, hp, plane, rows, :] = o * (1.0 / den)
            l_ref[0, hp, plane, rows, :] = m + jnp.log2(den)

    for n in range(g.nsets + 2):
        if n >= 2:
            finish(n - 2)
        if 1 <= n <= g.nsets:
            softmax(g, n - 1, (n - 1) % 2)
        if n < g.nsets:
            scores(g, n, n % 2)


def _halo_specs(d, rows_per_tile, seq_rows, width):
    per_tile = rows_per_tile // RADIUS
    n_halo = seq_rows // RADIUS
    return [
        pl.BlockSpec((1, d, RADIUS, width), lambda bi, i: (bi, 0, jnp.maximum(i * per_tile - 1, 0), 0)),
        pl.BlockSpec((1, d, RADIUS, width), lambda bi, i: (bi, 0, jnp.minimum((i + 1) * per_tile, n_halo - 1), 0)),
    ]


def _attention_group(kvq, gi):
    b, d, rows, _ = kvq.shape
    s = rows * d
    tt = T_ATTN
    tq = 2 * RADIUS
    out_spec = pl.BlockSpec((1, N_PAIR, PLANES, tt // PLANES, LANES), lambda bi, i: (bi, 0, 0, i, 0))
    out_sds = jax.ShapeDtypeStruct((b, N_PAIR, PLANES, s // PLANES, LANES), F32)
    return pl.pallas_call(
        functools.partial(_attn_group_kernel, gi=gi),
        grid=(b, s // tt),
        in_specs=[pl.BlockSpec((1, d, tt // d, QKV_W), lambda bi, i: (bi, 0, i, 0)),
                  *_halo_specs(d, tt // d, rows, 2 * ATTN_W)],
        out_specs=(out_spec, out_spec),
        out_shape=(out_sds, out_sds),
        scratch_shapes=[
            pltpu.VMEM((N_PAIR, 2 * tq, tq + 2 * RADIUS), F32),
            pltpu.VMEM((2, N_PAIR, 2 * tq, tq + 2 * RADIUS), F32),
            pltpu.VMEM((2, N_PAIR, 2 * tq, tq + 2 * RADIUS), BF16),
            pltpu.VMEM((2, 2, N_PAIR, tq, LANES), F32),
        ],
        compiler_params=pltpu.CompilerParams(
            dimension_semantics=("arbitrary", "arbitrary"), vmem_limit_bytes=VMEM_LIMIT),
        name=f"attn{gi}",
    )(kvq, kvq, kvq)


def _out_kernel(x_ref, c0_ref, p0_ref, n0_ref, o1_ref, l1_ref, o2_ref, l2_ref,
                za_ref, u_ref, up_ref, un_ref, pre_ref, hb_ref,
                cw_ref, wa_ref, wc_ref, wga_ref, wgc_ref, bg_ref, wo_ref, fg_ref, y_ref,
                ubuf_ref, b0_ref, s_scr, p_scr, st_scr, attn_scr):
    bi, ti = pl.program_id(0), pl.program_id(1)
    first, last = ti == 0, ti == pl.num_programs(1) - 1
    t = x_ref.shape[1]
    pad = ubuf_ref.shape[0] - t
    lo = pad // 2

    @pl.when(jnp.logical_and(bi == 0, ti == 0))
    def _init_bias():
        _bias_table(b0_ref, 0)

    g0 = _Group(0, c0_ref, p0_ref, n0_ref, b0_ref, t, first, last)
    scores, softmax, values = _make_stages(s_scr, p_scr, st_scr)

    per = g0.tq // PLANES

    def planes(ref, hp, n):
        return jnp.concatenate([ref[0, hp, c, n * per:(n + 1) * per, :] for c in range(PLANES)], axis=0)

    def finish(n, slot):
        for hp, o, m, d in values(g0, n, slot):
            l1, l2 = planes(l1_ref, hp, n), planes(l2_ref, hp, n)
            top = jnp.maximum(jnp.maximum(m, l1), l2)
            e0, e1, e2 = jnp.exp2(m - top), jnp.exp2(l1 - top), jnp.exp2(l2 - top)
            num = e0 * o + e1 * planes(o1_ref, hp, n) + e2 * planes(o2_ref, hp, n)
            res = num * (1.0 / (e0 * d + e1 + e2))
            for c in range(PLANES):
                attn_scr[hp, pl.ds(n * g0.tq + c, per, stride=PLANES), :] = res[c * per:(c + 1) * per]

    hb = hb_ref[0]
    half = D_MODEL // 2

    def gate_piece(w_ref, j):
        cols = slice(j * half, (j + 1) * half)
        return jnp.dot(hb, w_ref[:, cols], preferred_element_type=F32)

    def conv_branch():
        ubuf_ref[0:lo, :] = jnp.where(first, 0.0, up_ref[0].astype(F32))
        ubuf_ref[lo:lo + t, :] = u_ref[0].astype(F32)
        ubuf_ref[lo + t:, :] = jnp.where(last, 0.0, un_ref[0].astype(F32))
        conv = (cw_ref[0:1, :] * ubuf_ref[lo - 1:lo - 1 + t, :]
                + cw_ref[1:2, :] * ubuf_ref[lo:lo + t, :]
                + cw_ref[2:3, :] * ubuf_ref[lo + 1:lo + 1 + t, :])
        c_in = (pre_ref[0].astype(F32) * conv).astype(BF16)
        return jnp.dot(c_in, wc_ref[...], preferred_element_type=F32)

    fillers = [conv_branch, lambda: gate_piece(wga_ref, 0), lambda: gate_piece(wga_ref, 1),
               lambda: gate_piece(wgc_ref, 0), lambda: gate_piece(wgc_ref, 1)]
    filled = []
    assert g0.nsets + 2 >= len(fillers)
    for n in range(g0.nsets + 2):
        if n >= 2:
            finish(n - 2, n % 2)
        if 1 <= n <= g0.nsets:
            softmax(g0, n - 1, 1 - n % 2)
        if n < g0.nsets:
            scores(g0, n, n % 2)
        if n < len(fillers):
            filled.append(fillers[n]())
    c, ga0, ga1, gc0, gc1 = filled
    gate_a = _sigmoid(jnp.concatenate([ga0, ga1], axis=1) + bg_ref[:, 0:D_MODEL])
    gate_c = _sigmoid(jnp.concatenate([gc0, gc1], axis=1) + bg_ref[:, D_MODEL:])

    attn = jnp.concatenate([attn_scr[hp] for hp in range(N_PAIR)], axis=1)
    a_in = (za_ref[0].astype(F32) * attn).astype(BF16)
    a = jnp.dot(a_in, wa_ref[...], preferred_element_type=F32)
    merged = gate_a * a + gate_c * c
    h = x_ref[0] + jnp.dot(merged.astype(BF16), wo_ref[...], preferred_element_type=F32)
    ms = jnp.mean(h * h, axis=-1, keepdims=True)
    y_ref[0] = h * lax.rsqrt(ms + RMS_EPS) * fg_ref[...]


def _output(x, kvq0, groups12, za, u, pre, hb, conv_w, w_bf, b_gate, wa_bf, wc_bf, wo_bf, final_g):
    b, s, _ = x.shape
    t = T_OUT
    halo = 8
    tq0 = 2 * RADIUS
    tok = lambda width: pl.BlockSpec((1, t, width), lambda bi, i: (bi, i, 0))
    slabs = pl.BlockSpec((1, N_PAIR, PLANES, t // PLANES, LANES), lambda bi, i: (bi, 0, 0, i, 0))
    const = lambda shape: pl.BlockSpec(shape, lambda bi, i: (0,) * len(shape), pipeline_mode=pl.Buffered(1))
    gate_w = lambda j: pl.BlockSpec((D_MODEL, D_MODEL), lambda bi, i: (0, _G // D_MODEL + j),
                                    pipeline_mode=pl.Buffered(1))
    assert _G % D_MODEL == 0
    per_tile = t // halo
    n_halo = s // halo
    return pl.pallas_call(
        _out_kernel,
        grid=(b, s // t),
        in_specs=[
            tok(D_MODEL),
            pl.BlockSpec((1, 1, t, QKV_W), lambda bi, i: (bi, 0, i, 0)), *_halo_specs(1, t, s, 2 * ATTN_W),
            slabs, slabs, slabs, slabs,
            tok(ATTN_W), tok(CONV_W),
            pl.BlockSpec((1, halo, CONV_W), lambda bi, i: (bi, jnp.maximum(i * per_tile - 1, 0), 0)),
            pl.BlockSpec((1, halo, CONV_W), lambda bi, i: (bi, jnp.minimum((i + 1) * per_tile, n_halo - 1), 0)),
            tok(CONV_W), tok(D_MODEL),
            const((3, CONV_W)), const((ATTN_W, D_MODEL)), const((CONV_W, D_MODEL)),
            gate_w(0), gate_w(1), const((1, 2 * D_MODEL)),
            const((D_MODEL, D_MODEL)), const((1, D_MODEL)),
        ],
        out_specs=tok(D_MODEL),
        out_shape=jax.ShapeDtypeStruct((b, s, D_MODEL), F32),
        scratch_shapes=[
            pltpu.VMEM((t + 2 * halo, CONV_W), F32),
            pltpu.VMEM((N_PAIR, 2 * tq0, tq0 + 2 * RADIUS), F32),
            pltpu.VMEM((2, N_PAIR, 2 * tq0, tq0 + 2 * RADIUS), F32),
            pltpu.VMEM((2, N_PAIR, 2 * tq0, tq0 + 2 * RADIUS), BF16),
            pltpu.VMEM((2, 2, N_PAIR, tq0, LANES), F32),
            pltpu.VMEM((N_PAIR, t, LANES), F32),
        ],
        compiler_params=pltpu.CompilerParams(
            dimension_semantics=("arbitrary", "arbitrary"), vmem_limit_bytes=VMEM_LIMIT),
        name="out",
    )(x, kvq0, kvq0, kvq0, *groups12, za, u, u, u, pre, hb, conv_w, wa_bf, wc_bf, w_bf, w_bf,
      b_gate.reshape(1, 2 * D_MODEL), wo_bf, final_g.reshape(1, D_MODEL))


def _layer_and_norm(x, norm_g, w_bf, b_gate, conv_w, wa_bf, wc_bf, wo_bf, final_g):
    kvq0, kvq1, kvq2, za, u, pre, hb = _projection(x, norm_g, w_bf)
    groups12 = (*_attention_group(kvq1, 1), *_attention_group(kvq2, 2))
    return _output(x, kvq0, groups12, za, u, pre, hb, conv_w, w_bf, b_gate, wa_bf, wc_bf, wo_bf, final_g)


def kernel(x_prompt, x_sample, norm_g, w_in, b_gate, conv_w, w_attn_out, w_conv_out, w_o, final_g):
    assert norm_g.shape[0] == 1, "single layer"
    args = (norm_g[0], w_in[0].astype(BF16), b_gate[0], conv_w[0], w_attn_out[0].astype(BF16),
            w_conv_out[0].astype(BF16), w_o[0].astype(BF16), final_g)
    return (_layer_and_norm(x_prompt, *args), _layer_and_norm(x_sample, *args))
```

```python
import functools
import math

import jax
import jax.numpy as jnp
import numpy as np
from jax import lax
from jax.experimental import pallas as pl
from jax.experimental.pallas import tpu as pltpu

F32 = jnp.float32
BF16 = jnp.bfloat16

D_MODEL = 1024
N_GROUPS = 3
DILATIONS = (1, 4, 16)
WINDOWS = (128, 512, 2048)
HEADS = 8
HEAD_DIM = 64
ATTN_W = HEADS * HEAD_DIM
QKV_W = N_GROUPS * ATTN_W
CONV_W = 512
IN_W = 3 * QKV_W + ATTN_W + 4 * CONV_W + 2 * D_MODEL
RADIUS = 64
RMS_EPS = 1e-6
NEG_BIG = -1e30
LOG2E = math.log2(math.e)
Q_SCALE = LOG2E / math.sqrt(HEAD_DIM)

LANES = 128
N_SLAB = D_MODEL // LANES
N_PAIR = ATTN_W // LANES

_Q0, _K0, _V0 = 0, QKV_W, 2 * QKV_W
_ZA = 3 * QKV_W
_HC = _ZA + ATTN_W
_GB = _HC + CONV_W
_GC = _GB + CONV_W
_ZB = _GC + CONV_W
_G = _ZB + CONV_W

T_PROJ = 512
T_ATTN = 2048
T_OUT = 512
PLANES = 4
assert all(d == 1 or d % PLANES == 0 for d in DILATIONS) and T_PROJ == T_OUT
VMEM_LIMIT = 56 * 1024 * 1024


def _alibi_slopes():
    n = N_GROUPS * HEADS
    s = 2.0 ** (-8.0 * np.arange(1, n + 1) / n)
    return s.astype(np.float32).reshape(N_GROUPS, HEADS)


def _sigmoid(z):
    return 1.0 / (1.0 + jnp.exp(-z))


def _proj_kernel(*refs, bounds):
    x_refs = refs[:len(bounds)]
    g_ref, w_ref, kvq0_ref, kvq1_ref, kvq2_ref, za_ref, u_ref, pre_ref, hb_ref, hslab_ref = refs[len(bounds):]
    t = hb_ref.shape[1]
    step = pl.program_id(0)
    x = x_refs[-1][0]
    for x_ref, (_, hi) in reversed(list(zip(x_refs[:-1], bounds[:-1]))):
        x = jnp.where(step < hi, x_ref[0], x)
    ms = jnp.mean(x * x, axis=-1, keepdims=True)
    hn = x * lax.rsqrt(ms + RMS_EPS) * g_ref[...]
    hb = hn.astype(BF16)
    hb_ref[0] = hb

    def mm(h, c0, width):
        return jnp.dot(h, w_ref[:, c0:c0 + width], preferred_element_type=F32)

    kvq0_ref[0, 0, :, 0:ATTN_W] = mm(hb, _K0, ATTN_W).astype(BF16)
    kvq0_ref[0, 0, :, ATTN_W:2 * ATTN_W] = mm(hb, _V0, ATTN_W).astype(BF16)

    for s in range(N_SLAB):
        hslab_ref[s] = hn[:, s * LANES:(s + 1) * LANES]
    for gi, out_ref in ((1, kvq1_ref), (2, kvq2_ref)):
        d = DILATIONS[gi]
        rows = t // d
        parts = []
        for r in range(d):
            parts.append(jnp.concatenate(
                [hslab_ref[s, pl.ds(r, rows, stride=d), :] for s in range(N_SLAB)], axis=-1))
        hr = jnp.concatenate(parts, axis=0).astype(BF16)
        if d == PLANES:
            kvq0_ref[0, 0, :, 2 * ATTN_W:] = (mm(hr, _Q0, ATTN_W) * Q_SCALE).astype(BF16)
        for c, (col, scale) in enumerate(((_K0, None), (_V0, None), (_Q0, Q_SCALE))):
            res = mm(hr, col + gi * ATTN_W, ATTN_W)
            if scale is not None:
                res = res * scale
            res = res.astype(BF16)
            for r in range(d):
                out_ref[0, r, :, c * ATTN_W:(c + 1) * ATTN_W] = res[r * rows:(r + 1) * rows]

    za = mm(hb, _ZA, ATTN_W)
    za_ref[0] = (za * _sigmoid(za)).astype(BF16)
    u_ref[0] = (mm(hb, _GC, CONV_W) * mm(hb, _HC, CONV_W)).astype(BF16)
    zb = mm(hb, _ZB, CONV_W)
    pre_ref[0] = (zb * _sigmoid(zb) * mm(hb, _GB, CONV_W)).astype(BF16)


def _projection(xs, norm_g, w_bf):
    t = T_PROJ
    bounds, x_specs, lo = [], [], 0
    for x in xs:
        b, s, _ = x.shape
        per_row, n = s // t, b * (s // t)

        def x_map(i, lo=lo, n=n, per_row=per_row):
            j = jnp.clip(i - lo, 0, n - 1)
            return j // per_row, j % per_row, 0

        x_specs.append(pl.BlockSpec((1, t, D_MODEL), x_map))
        bounds.append((lo, lo + n))
        lo += n
    s = lo * t
    tok = lambda width: pl.BlockSpec((1, t, width), lambda i: (0, i, 0))
    res = lambda d: pl.BlockSpec((1, d, t // d, QKV_W), lambda i: (0, 0, i, 0))
    const = lambda shape: pl.BlockSpec(shape, lambda i: (0,) * len(shape))
    out_shape = (
        jax.ShapeDtypeStruct((1, 1, s, QKV_W), BF16),
        jax.ShapeDtypeStruct((1, 4, s // 4, QKV_W), BF16),
        jax.ShapeDtypeStruct((1, 16, s // 16, QKV_W), BF16),
        jax.ShapeDtypeStruct((1, s, ATTN_W), BF16),
        jax.ShapeDtypeStruct((1, s, CONV_W), BF16),
        jax.ShapeDtypeStruct((1, s, CONV_W), BF16),
        jax.ShapeDtypeStruct((1, s, D_MODEL), BF16),
    )
    return pl.pallas_call(
        functools.partial(_proj_kernel, bounds=tuple(bounds)),
        grid=(lo,),
        in_specs=[
            *x_specs,
            const((1, D_MODEL)),
            pl.BlockSpec((D_MODEL, _G), lambda i: (0, 0), pipeline_mode=pl.Buffered(1)),
        ],
        out_specs=(res(1), res(4), res(16), tok(ATTN_W), tok(CONV_W), tok(CONV_W), tok(D_MODEL)),
        out_shape=out_shape,
        scratch_shapes=[pltpu.VMEM((N_SLAB, t, LANES), F32)],
        compiler_params=pltpu.CompilerParams(
            dimension_semantics=("arbitrary",), vmem_limit_bytes=VMEM_LIMIT),
        name="proj",
    )(*xs, norm_g.reshape(1, D_MODEL), w_bf)


class _Group:
    def __init__(self, gi, cur, prev, nxt, bias_ref, tt, first, last):
        self.d = DILATIONS[gi]
        self.cur, self.prev, self.nxt, self.bias_ref = cur, prev, nxt, bias_ref
        self.tq = bias_ref.shape[1] // 2
        self.tk = bias_ref.shape[2]
        self.rows = tt // self.d
        self.bpr = self.rows // self.tq
        self.nsets = self.d * self.bpr
        self.first, self.last = first, last
        assert self.tk == self.tq + 2 * RADIUS and self.rows % self.tq == 0 and self.tq % PLANES == 0

    def _split(self, n):
        return n // self.bpr, n % self.bpr

    def q(self, n):
        r, blk = self._split(n)
        cols = slice(2 * ATTN_W, QKV_W)
        if self.d > 1:
            return self.cur[0, r, blk * self.tq:(blk + 1) * self.tq, cols]
        per, size = self.tq // PLANES, self.rows // PLANES
        return jnp.concatenate(
            [self.cur[0, r, c * size + blk * per:c * size + (blk + 1) * per, cols] for c in range(PLANES)], axis=0)

    def window(self, n, cols):
        r, blk = self._split(n)
        lo, hi = blk * self.tq - RADIUS, (blk + 1) * self.tq + RADIUS
        parts = [self.cur[0, r, max(lo, 0):min(hi, self.rows), cols]]
        if lo < 0:
            parts.insert(0, self.prev[0, r, :, cols])
        if hi > self.rows:
            parts.append(self.nxt[0, r, :, cols])
        return jnp.concatenate(parts, axis=0) if len(parts) > 1 else parts[0]

    def col_ok(self, n):
        _, blk = self._split(n)
        col = lax.broadcasted_iota(jnp.int32, (1, self.tk), 1)
        bad = []
        if blk == 0:
            bad.append(jnp.logical_and(col < RADIUS, self.first))
        if blk == self.bpr - 1:
            bad.append(jnp.logical_and(col >= self.tk - RADIUS, self.last))
        if not bad:
            return None
        return jnp.logical_not(functools.reduce(jnp.logical_or, bad))

    def plane_rows(self, n):
        r, blk = self._split(n)
        step = self.d // PLANES
        start = r // PLANES + step * blk * self.tq
        return r % PLANES, (pl.ds(start, self.tq) if step == 1 else pl.ds(start, self.tq, stride=step))


def _bias_table(bias_ref, gi):
    slopes = _alibi_slopes()
    tq, tk = bias_ref.shape[1] // 2, bias_ref.shape[2]
    qi = lax.broadcasted_iota(jnp.int32, (tq, tk), 0)
    if DILATIONS[gi] == 1:
        per = tq // PLANES
        shift = per.bit_length() - 1
        assert per == 1 << shift
        qi = (qi & (per - 1)) * PLANES + (qi >> shift)
    kj = lax.broadcasted_iota(jnp.int32, (tq, tk), 1)
    rel = jnp.abs(kj - RADIUS - qi)
    dist = (rel * DILATIONS[gi]).astype(F32)
    for h in range(HEADS):
        rows = slice((h % 2) * tq, (h % 2 + 1) * tq)
        bias_ref[h // 2, rows, :] = jnp.where(rel <= RADIUS, (-LOG2E * float(slopes[gi, h])) * dist, NEG_BIG)


def _make_stages(s_scr, p_scr, st_scr):
    low = lax.broadcasted_iota(jnp.int32, (1, LANES), 1) < HEAD_DIM
    zero = jnp.zeros((), BF16)

    def pick(a, tq):
        return jnp.where(low, jnp.broadcast_to(a[:tq], (tq, LANES)), jnp.broadcast_to(a[tq:], (tq, LANES)))

    def scores(g, n, slot):
        q, k = g.q(n), g.window(n, slice(0, ATTN_W))
        for hp in range(N_PAIR):
            sl = slice(hp * LANES, (hp + 1) * LANES)
            qp = q[:, sl]
            q2 = jnp.concatenate([jnp.where(low, qp, zero), jnp.where(low, zero, qp)], axis=0)
            s_scr[slot, hp, 0:2 * g.tq, 0:g.tk] = lax.dot_general(
                q2, k[:, sl], (((1,), (1,)), ((), ())), preferred_element_type=F32)

    def softmax(g, n, slot):
        ok = g.col_ok(n)
        for hp in range(N_PAIR):
            s = s_scr[slot, hp, 0:2 * g.tq, 0:g.tk] + g.bias_ref[hp]
            if ok is not None:
                s = jnp.where(ok, s, NEG_BIG)
            m = jnp.max(s, axis=-1, keepdims=True)
            p = jnp.exp2(s - m)
            den = jnp.sum(p, axis=-1, keepdims=True)
            p_scr[slot, hp, 0:2 * g.tq, 0:g.tk] = p.astype(BF16)
            st_scr[slot, 0, hp, 0:g.tq] = pick(m, g.tq)
            st_scr[slot, 1, hp, 0:g.tq] = pick(den, g.tq)

    def values(g, n, slot):
        v = g.window(n, slice(ATTN_W, 2 * ATTN_W))
        for hp in range(N_PAIR):
            o = jnp.dot(p_scr[slot, hp, 0:2 * g.tq, 0:g.tk], v[:, hp * LANES:(hp + 1) * LANES],
                        preferred_element_type=F32)
            yield hp, pick(o, g.tq), st_scr[slot, 0, hp, 0:g.tq], st_scr[slot, 1, hp, 0:g.tq]

    return scores, softmax, values


def _attn_group_kernel(cur_ref, prev_ref, nxt_ref, o_ref, l_ref, bias_ref, s_scr, p_scr, st_scr, *, gi, starts):
    bi, ti = pl.program_id(0), pl.program_id(1)
    ends = tuple(s - 1 for s in starts[1:]) + (pl.num_programs(1) - 1,)
    first = functools.reduce(jnp.logical_or, [ti == s for s in starts])
    last = functools.reduce(jnp.logical_or, [ti == e for e in ends])
    tt = PLANES * o_ref.shape[3]

    @pl.when(jnp.logical_and(bi == 0, ti == 0))
    def _init_bias():
        _bias_table(bias_ref, gi)

    g = _Group(gi, cur_ref, prev_ref, nxt_ref, bias_ref, tt, first, last)
    scores, softmax, values = _make_stages(s_scr, p_scr, st_scr)

    def finish(n):
        plane, rows = g.plane_rows(n)
        for hp, o, m, den in values(g, n, n % 2):
            o_ref[0, hp, plane, rows, :] = o * (1.0 / den)
            l_ref[0, hp, plane, rows, :] = m + jnp.log2(den)

    for n in range(g.nsets + 2):
        if n >= 2:
            finish(n - 2)
        if 1 <= n <= g.nsets:
            softmax(g, n - 1, (n - 1) % 2)
        if n < g.nsets:
            scores(g, n, n % 2)


def _halo_specs(d, rows_per_tile, total_rows, width, tile):
    per_tile = rows_per_tile // RADIUS
    n_halo = total_rows // RADIUS
    return [
        pl.BlockSpec((1, d, RADIUS, width), lambda bi, i: (0, 0, jnp.maximum(tile(bi, i) * per_tile - 1, 0), 0)),
        pl.BlockSpec((1, d, RADIUS, width),
                     lambda bi, i: (0, 0, jnp.minimum((tile(bi, i) + 1) * per_tile, n_halo - 1), 0)),
    ]


def _attention_group(kvq, gi, seq_lens):
    b, d, rows, _ = kvq.shape
    s = rows * d
    tt = T_ATTN
    tq = 2 * RADIUS
    assert b == 1 and s == sum(seq_lens) and all(n % tt == 0 for n in seq_lens)
    starts = tuple(sum(seq_lens[:k]) // tt for k in range(len(seq_lens)))
    out_spec = pl.BlockSpec((1, N_PAIR, PLANES, tt // PLANES, LANES), lambda bi, i: (bi, 0, 0, i, 0))
    out_sds = jax.ShapeDtypeStruct((b, N_PAIR, PLANES, s // PLANES, LANES), F32)
    return pl.pallas_call(
        functools.partial(_attn_group_kernel, gi=gi, starts=starts),
        grid=(b, s // tt),
        in_specs=[pl.BlockSpec((1, d, tt // d, QKV_W), lambda bi, i: (bi, 0, i, 0)),
                  *_halo_specs(d, tt // d, rows, 2 * ATTN_W, lambda bi, i: i)],
        out_specs=(out_spec, out_spec),
        out_shape=(out_sds, out_sds),
        scratch_shapes=[
            pltpu.VMEM((N_PAIR, 2 * tq, tq + 2 * RADIUS), F32),
            pltpu.VMEM((2, N_PAIR, 2 * tq, tq + 2 * RADIUS), F32),
            pltpu.VMEM((2, N_PAIR, 2 * tq, tq + 2 * RADIUS), BF16),
            pltpu.VMEM((2, 2, N_PAIR, tq, LANES), F32),
        ],
        compiler_params=pltpu.CompilerParams(
            dimension_semantics=("arbitrary", "arbitrary"), vmem_limit_bytes=VMEM_LIMIT),
        name=f"attn{gi}",
    )(kvq, kvq, kvq)


def _out_kernel(x_ref, c0_ref, p0_ref, n0_ref, o1_ref, l1_ref, o2_ref, l2_ref,
                za_ref, u_ref, up_ref, un_ref, pre_ref, hb_ref,
                cw_ref, wa_ref, wc_ref, wga_ref, wgc_ref, bg_ref, wo_ref, fg_ref, y_ref,
                ubuf_ref, b0_ref, s_scr, p_scr, st_scr, attn_scr):
    bi, ti = pl.program_id(0), pl.program_id(1)
    first, last = ti == 0, ti == pl.num_programs(1) - 1
    t = x_ref.shape[1]
    pad = ubuf_ref.shape[0] - t
    lo = pad // 2

    @pl.when(jnp.logical_and(bi == 0, ti == 0))
    def _init_bias():
        _bias_table(b0_ref, 0)

    g0 = _Group(0, c0_ref, p0_ref, n0_ref, b0_ref, t, first, last)
    scores, softmax, values = _make_stages(s_scr, p_scr, st_scr)

    per = g0.tq // PLANES

    def planes(ref, hp, n):
        return jnp.concatenate([ref[0, hp, c, n * per:(n + 1) * per, :] for c in range(PLANES)], axis=0)

    def finish(n, slot):
        for hp, o, m, d in values(g0, n, slot):
            l1, l2 = planes(l1_ref, hp, n), planes(l2_ref, hp, n)
            top = jnp.maximum(jnp.maximum(m, l1), l2)
            e0, e1, e2 = jnp.exp2(m - top), jnp.exp2(l1 - top), jnp.exp2(l2 - top)
            num = e0 * o + e1 * planes(o1_ref, hp, n) + e2 * planes(o2_ref, hp, n)
            res = num * (1.0 / (e0 * d + e1 + e2))
            for c in range(PLANES):
                attn_scr[hp, pl.ds(n * g0.tq + c, per, stride=PLANES), :] = res[c * per:(c + 1) * per]

    hb = hb_ref[0]
    half = D_MODEL // 2

    def gate_piece(w_ref, j):
        cols = slice(j * half, (j + 1) * half)
        return jnp.dot(hb, w_ref[:, cols], preferred_element_type=F32)

    def conv_branch():
        ubuf_ref[0:lo, :] = jnp.where(first, 0.0, up_ref[0].astype(F32))
        ubuf_ref[lo:lo + t, :] = u_ref[0].astype(F32)
        ubuf_ref[lo + t:, :] = jnp.where(last, 0.0, un_ref[0].astype(F32))
        conv = (cw_ref[0:1, :] * ubuf_ref[lo - 1:lo - 1 + t, :]
                + cw_ref[1:2, :] * ubuf_ref[lo:lo + t, :]
                + cw_ref[2:3, :] * ubuf_ref[lo + 1:lo + 1 + t, :])
        c_in = (pre_ref[0].astype(F32) * conv).astype(BF16)
        return jnp.dot(c_in, wc_ref[...], preferred_element_type=F32)

    fillers = [conv_branch, lambda: gate_piece(wga_ref, 0), lambda: gate_piece(wga_ref, 1),
               lambda: gate_piece(wgc_ref, 0), lambda: gate_piece(wgc_ref, 1)]
    filled = []
    assert g0.nsets + 2 >= len(fillers)
    for n in range(g0.nsets + 2):
        if n >= 2:
            finish(n - 2, n % 2)
        if 1 <= n <= g0.nsets:
            softmax(g0, n - 1, 1 - n % 2)
        if n < g0.nsets:
            scores(g0, n, n % 2)
        if n < len(fillers):
            filled.append(fillers[n]())
    c, ga0, ga1, gc0, gc1 = filled
    gate_a = _sigmoid(jnp.concatenate([ga0, ga1], axis=1) + bg_ref[:, 0:D_MODEL])
    gate_c = _sigmoid(jnp.concatenate([gc0, gc1], axis=1) + bg_ref[:, D_MODEL:])

    attn = jnp.concatenate([attn_scr[hp] for hp in range(N_PAIR)], axis=1)
    a_in = (za_ref[0].astype(F32) * attn).astype(BF16)
    a = jnp.dot(a_in, wa_ref[...], preferred_element_type=F32)
    merged = gate_a * a + gate_c * c
    h = x_ref[0] + jnp.dot(merged.astype(BF16), wo_ref[...], preferred_element_type=F32)
    ms = jnp.mean(h * h, axis=-1, keepdims=True)
    y_ref[0] = h * lax.rsqrt(ms + RMS_EPS) * fg_ref[...]


def _output(x, first_token, kvq0, groups12, za, u, pre, hb, conv_w, w_bf, b_gate, wa_bf, wc_bf, wo_bf, final_g):
    b, s, _ = x.shape
    t = T_OUT
    halo = 8
    tq0 = 2 * RADIUS
    n = s // t
    tile = lambda bi, i: first_token // t + bi * n + i
    flat = lambda width: pl.BlockSpec((1, t, width), lambda bi, i: (0, tile(bi, i), 0))
    tok = lambda width: pl.BlockSpec((1, t, width), lambda bi, i: (bi, i, 0))
    slabs = pl.BlockSpec((1, N_PAIR, PLANES, t // PLANES, LANES), lambda bi, i: (0, 0, 0, tile(bi, i), 0))
    const = lambda shape: pl.BlockSpec(shape, lambda bi, i: (0,) * len(shape), pipeline_mode=pl.Buffered(1))
    gate_w = lambda j: pl.BlockSpec((D_MODEL, D_MODEL), lambda bi, i: (0, _G // D_MODEL + j),
                                    pipeline_mode=pl.Buffered(1))
    assert _G % D_MODEL == 0
    per_tile = t // halo
    n_halo = u.shape[1] // halo
    assert first_token % t == 0
    return pl.pallas_call(
        _out_kernel,
        grid=(b, n),
        in_specs=[
            tok(D_MODEL),
            pl.BlockSpec((1, 1, t, QKV_W), lambda bi, i: (0, 0, tile(bi, i), 0)),
            *_halo_specs(1, t, kvq0.shape[2], 2 * ATTN_W, tile),
            slabs, slabs, slabs, slabs,
            flat(ATTN_W), flat(CONV_W),
            pl.BlockSpec((1, halo, CONV_W), lambda bi, i: (0, jnp.maximum(tile(bi, i) * per_tile - 1, 0), 0)),
            pl.BlockSpec((1, halo, CONV_W),
                         lambda bi, i: (0, jnp.minimum((tile(bi, i) + 1) * per_tile, n_halo - 1), 0)),
            flat(CONV_W), flat(D_MODEL),
            const((3, CONV_W)), const((ATTN_W, D_MODEL)), const((CONV_W, D_MODEL)),
            gate_w(0), gate_w(1), const((1, 2 * D_MODEL)),
            const((D_MODEL, D_MODEL)), const((1, D_MODEL)),
        ],
        out_specs=tok(D_MODEL),
        out_shape=jax.ShapeDtypeStruct((b, s, D_MODEL), F32),
        scratch_shapes=[
            pltpu.VMEM((t + 2 * halo, CONV_W), F32),
            pltpu.VMEM((N_PAIR, 2 * tq0, tq0 + 2 * RADIUS), F32),
            pltpu.VMEM((2, N_PAIR, 2 * tq0, tq0 + 2 * RADIUS), F32),
            pltpu.VMEM((2, N_PAIR, 2 * tq0, tq0 + 2 * RADIUS), BF16),
            pltpu.VMEM((2, 2, N_PAIR, tq0, LANES), F32),
            pltpu.VMEM((N_PAIR, t, LANES), F32),
        ],
        compiler_params=pltpu.CompilerParams(
            dimension_semantics=("arbitrary", "arbitrary"), vmem_limit_bytes=VMEM_LIMIT),
        name="out",
    )(x, kvq0, kvq0, kvq0, *groups12, za, u, u, u, pre, hb, conv_w, wa_bf, wc_bf, w_bf, w_bf,
      b_gate.reshape(1, 2 * D_MODEL), wo_bf, final_g.reshape(1, D_MODEL))


def _layers_and_norm(xs, norm_g, w_bf, b_gate, conv_w, wa_bf, wc_bf, wo_bf, final_g):
    kvq0, kvq1, kvq2, za, u, pre, hb = _projection(xs, norm_g, w_bf)
    seq_lens = tuple(x.shape[1] for x in xs for _ in range(x.shape[0]))
    groups12 = (*_attention_group(kvq1, 1, seq_lens), *_attention_group(kvq2, 2, seq_lens))
    outs, first_token = [], 0
    for x in xs:
        outs.append(_output(x, first_token, kvq0, groups12, za, u, pre, hb, conv_w, w_bf, b_gate,
                            wa_bf, wc_bf, wo_bf, final_g))
        first_token += x.shape[0] * x.shape[1]
    return tuple(outs)


def kernel(x_prompt, x_sample, norm_g, w_in, b_gate, conv_w, w_attn_out, w_conv_out, w_o, final_g):
    assert norm_g.shape[0] == 1, "single layer"
    return _layers_and_norm(
        (x_prompt, x_sample), norm_g[0], w_in[0].astype(BF16), b_gate[0], conv_w[0], w_attn_out[0].astype(BF16),
        w_conv_out[0].astype(BF16), w_o[0].astype(BF16), final_g)
```

```python
import functools
import math

import jax
import jax.numpy as jnp
import numpy as np
from jax import lax
from jax.experimental import pallas as pl
from jax.experimental.pallas import tpu as pltpu

F32 = jnp.float32
BF16 = jnp.bfloat16

D_MODEL = 1024
N_GROUPS = 3
DILATIONS = (1, 4, 16)
WINDOWS = (128, 512, 2048)
HEADS = 8
HEAD_DIM = 64
ATTN_W = HEADS * HEAD_DIM
QKV_W = N_GROUPS * ATTN_W
CONV_W = 512
IN_W = 3 * QKV_W + ATTN_W + 4 * CONV_W + 2 * D_MODEL
RADIUS = 64
RMS_EPS = 1e-6
NEG_BIG = -1e30
LOG2E = math.log2(math.e)
Q_SCALE = LOG2E / math.sqrt(HEAD_DIM)

LANES = 128
N_SLAB = D_MODEL // LANES
N_PAIR = ATTN_W // LANES

_Q0, _K0, _V0 = 0, QKV_W, 2 * QKV_W
_ZA = 3 * QKV_W
_HC = _ZA + ATTN_W
_GB = _HC + CONV_W
_GC = _GB + CONV_W
_ZB = _GC + CONV_W
_G = _ZB + CONV_W

T_PROJ = 512
W_CHUNK = 512
T_ATTN = 2048
T_OUT = 512
PLANES = 4
assert all(d == 1 or d % PLANES == 0 for d in DILATIONS) and T_PROJ == T_OUT
VMEM_LIMIT = 56 * 1024 * 1024


def _alibi_slopes():
    n = N_GROUPS * HEADS
    s = 2.0 ** (-8.0 * np.arange(1, n + 1) / n)
    return s.astype(np.float32).reshape(N_GROUPS, HEADS)


def _sigmoid(z):
    return 1.0 / (1.0 + jnp.exp(-z))


def _proj_kernel(*refs, bounds, n_warm):
    nb = len(bounds)
    wf_ref, w_scr = refs[nb + 1], refs[-1]
    step = pl.program_id(0)

    @pl.when(step < n_warm)
    def _cast_chunk():
        w_scr[step] = wf_ref[...].astype(BF16)

    @pl.when(step >= n_warm)
    def _tile():
        _proj_tile(*refs[:nb + 1], w_scr, *refs[nb + 2:-1], bounds=bounds, step=step - n_warm)


def _proj_tile(*refs, bounds, step):
    x_refs = refs[:len(bounds)]
    g_ref, w_ref, kvq0_ref, kvq1_ref, kvq2_ref, za_ref, u_ref, pre_ref, hb_ref, hslab_ref = refs[len(bounds):]
    t = hb_ref.shape[1]
    x = x_refs[-1][0]
    for x_ref, (_, hi) in reversed(list(zip(x_refs[:-1], bounds[:-1]))):
        x = jnp.where(step < hi, x_ref[0], x)
    ms = jnp.mean(x * x, axis=-1, keepdims=True)
    hn = x * lax.rsqrt(ms + RMS_EPS) * g_ref[...]
    hb = hn.astype(BF16)
    hb_ref[0] = hb

    def mm(h, c0, width):
        assert width == W_CHUNK and c0 % W_CHUNK == 0
        return jnp.dot(h, w_ref[c0 // W_CHUNK], preferred_element_type=F32)

    kvq0_ref[0, 0, :, 0:ATTN_W] = mm(hb, _K0, ATTN_W).astype(BF16)
    kvq0_ref[0, 0, :, ATTN_W:2 * ATTN_W] = mm(hb, _V0, ATTN_W).astype(BF16)

    for s in range(N_SLAB):
        hslab_ref[s] = hn[:, s * LANES:(s + 1) * LANES]
    for gi, out_ref in ((1, kvq1_ref), (2, kvq2_ref)):
        d = DILATIONS[gi]
        rows = t // d
        parts = []
        for r in range(d):
            parts.append(jnp.concatenate(
                [hslab_ref[s, pl.ds(r, rows, stride=d), :] for s in range(N_SLAB)], axis=-1))
        hr = jnp.concatenate(parts, axis=0).astype(BF16)
        if d == PLANES:
            kvq0_ref[0, 0, :, 2 * ATTN_W:] = (mm(hr, _Q0, ATTN_W) * Q_SCALE).astype(BF16)
        for c, (col, scale) in enumerate(((_K0, None), (_V0, None), (_Q0, Q_SCALE))):
            res = mm(hr, col + gi * ATTN_W, ATTN_W)
            if scale is not None:
                res = res * scale
            res = res.astype(BF16)
            for r in range(d):
                out_ref[0, r, :, c * ATTN_W:(c + 1) * ATTN_W] = res[r * rows:(r + 1) * rows]

    za = mm(hb, _ZA, ATTN_W)
    za_ref[0] = (za * _sigmoid(za)).astype(BF16)
    u_ref[0] = (mm(hb, _GC, CONV_W) * mm(hb, _HC, CONV_W)).astype(BF16)
    zb = mm(hb, _ZB, CONV_W)
    pre_ref[0] = (zb * _sigmoid(zb) * mm(hb, _GB, CONV_W)).astype(BF16)


def _projection(xs, norm_g, w_in):
    t = T_PROJ
    n_warm = _G // W_CHUNK
    bounds, x_specs, lo = [], [], 0
    for x in xs:
        b, s, _ = x.shape
        per_row, n = s // t, b * (s // t)

        def x_map(i, lo=lo, n=n, per_row=per_row):
            j = jnp.clip(i - n_warm - lo, 0, n - 1)
            return j // per_row, j % per_row, 0

        x_specs.append(pl.BlockSpec((1, t, D_MODEL), x_map))
        bounds.append((lo, lo + n))
        lo += n
    s = lo * t
    tile = lambda i: jnp.maximum(i - n_warm, 0)
    tok = lambda width: pl.BlockSpec((1, t, width), lambda i: (0, tile(i), 0))
    res = lambda d: pl.BlockSpec((1, d, t // d, QKV_W), lambda i: (0, 0, tile(i), 0))
    const = lambda shape: pl.BlockSpec(shape, lambda i: (0,) * len(shape))
    out_shape = (
        jax.ShapeDtypeStruct((1, 1, s, QKV_W), BF16),
        jax.ShapeDtypeStruct((1, 4, s // 4, QKV_W), BF16),
        jax.ShapeDtypeStruct((1, 16, s // 16, QKV_W), BF16),
        jax.ShapeDtypeStruct((1, s, ATTN_W), BF16),
        jax.ShapeDtypeStruct((1, s, CONV_W), BF16),
        jax.ShapeDtypeStruct((1, s, CONV_W), BF16),
        jax.ShapeDtypeStruct((1, s, D_MODEL), BF16),
    )
    return pl.pallas_call(
        functools.partial(_proj_kernel, bounds=tuple(bounds), n_warm=n_warm),
        grid=(n_warm + lo,),
        in_specs=[
            *x_specs,
            const((1, D_MODEL)),
            pl.BlockSpec((D_MODEL, W_CHUNK), lambda i: (0, jnp.minimum(i, n_warm - 1))),
        ],
        out_specs=(res(1), res(4), res(16), tok(ATTN_W), tok(CONV_W), tok(CONV_W), tok(D_MODEL)),
        out_shape=out_shape,
        scratch_shapes=[pltpu.VMEM((N_SLAB, t, LANES), F32),
                        pltpu.VMEM((n_warm, D_MODEL, W_CHUNK), BF16)],
        compiler_params=pltpu.CompilerParams(
            dimension_semantics=("arbitrary",), vmem_limit_bytes=VMEM_LIMIT),
        name="proj",
    )(*xs, norm_g.reshape(1, D_MODEL), w_in)


class _Group:
    def __init__(self, gi, cur, prev, nxt, bias_ref, tt, first, last):
        self.d = DILATIONS[gi]
        self.cur, self.prev, self.nxt, self.bias_ref = cur, prev, nxt, bias_ref
        self.tq = bias_ref.shape[1] // 2
        self.tk = bias_ref.shape[2]
        self.rows = tt // self.d
        self.bpr = self.rows // self.tq
        self.nsets = self.d * self.bpr
        self.first, self.last = first, last
        assert self.tk == self.tq + 2 * RADIUS and self.rows % self.tq == 0 and self.tq % PLANES == 0

    def _split(self, n):
        return n // self.bpr, n % self.bpr

    def q(self, n):
        r, blk = self._split(n)
        cols = slice(2 * ATTN_W, QKV_W)
        if self.d > 1:
            return self.cur[0, r, blk * self.tq:(blk + 1) * self.tq, cols]
        per, size = self.tq // PLANES, self.rows // PLANES
        return jnp.concatenate(
            [self.cur[0, r, c * size + blk * per:c * size + (blk + 1) * per, cols] for c in range(PLANES)], axis=0)

    def window(self, n, cols):
        r, blk = self._split(n)
        lo, hi = blk * self.tq - RADIUS, (blk + 1) * self.tq + RADIUS
        parts = [self.cur[0, r, max(lo, 0):min(hi, self.rows), cols]]
        if lo < 0:
            parts.insert(0, self.prev[0, r, :, cols])
        if hi > self.rows:
            parts.append(self.nxt[0, r, :, cols])
        return jnp.concatenate(parts, axis=0) if len(parts) > 1 else parts[0]

    def col_ok(self, n):
        _, blk = self._split(n)
        col = lax.broadcasted_iota(jnp.int32, (1, self.tk), 1)
        bad = []
        if blk == 0:
            bad.append(jnp.logical_and(col < RADIUS, self.first))
        if blk == self.bpr - 1:
            bad.append(jnp.logical_and(col >= self.tk - RADIUS, self.last))
        if not bad:
            return None
        return jnp.logical_not(functools.reduce(jnp.logical_or, bad))

    def plane_rows(self, n):
        r, blk = self._split(n)
        step = self.d // PLANES
        start = r // PLANES + step * blk * self.tq
        return r % PLANES, (pl.ds(start, self.tq) if step == 1 else pl.ds(start, self.tq, stride=step))


def _bias_table(bias_ref, gi):
    slopes = _alibi_slopes()
    tq, tk = bias_ref.shape[1] // 2, bias_ref.shape[2]
    qi = lax.broadcasted_iota(jnp.int32, (tq, tk), 0)
    if DILATIONS[gi] == 1:
        per = tq // PLANES
        shift = per.bit_length() - 1
        assert per == 1 << shift
        qi = (qi & (per - 1)) * PLANES + (qi >> shift)
    kj = lax.broadcasted_iota(jnp.int32, (tq, tk), 1)
    rel = jnp.abs(kj - RADIUS - qi)
    dist = (rel * DILATIONS[gi]).astype(F32)
    for h in range(HEADS):
        rows = slice((h % 2) * tq, (h % 2 + 1) * tq)
        bias_ref[h // 2, rows, :] = jnp.where(rel <= RADIUS, (-LOG2E * float(slopes[gi, h])) * dist, NEG_BIG)


def _make_stages(s_scr, p_scr, st_scr):
    low = lax.broadcasted_iota(jnp.int32, (1, LANES), 1) < HEAD_DIM
    zero = jnp.zeros((), BF16)

    def pick(a, tq):
        return jnp.where(low, jnp.broadcast_to(a[:tq], (tq, LANES)), jnp.broadcast_to(a[tq:], (tq, LANES)))

    def scores(g, n, slot):
        q, k = g.q(n), g.window(n, slice(0, ATTN_W))
        for hp in range(N_PAIR):
            sl = slice(hp * LANES, (hp + 1) * LANES)
            qp = q[:, sl]
            q2 = jnp.concatenate([jnp.where(low, qp, zero), jnp.where(low, zero, qp)], axis=0)
            s_scr[slot, hp, 0:2 * g.tq, 0:g.tk] = lax.dot_general(
                q2, k[:, sl], (((1,), (1,)), ((), ())), preferred_element_type=F32)

    def softmax(g, n, slot):
        ok = g.col_ok(n)
        for hp in range(N_PAIR):
            s = s_scr[slot, hp, 0:2 * g.tq, 0:g.tk] + g.bias_ref[hp]
            if ok is not None:
                s = jnp.where(ok, s, NEG_BIG)
            m = jnp.max(s, axis=-1, keepdims=True)
            p = jnp.exp2(s - m)
            den = jnp.sum(p, axis=-1, keepdims=True)
            p_scr[slot, hp, 0:2 * g.tq, 0:g.tk] = p.astype(BF16)
            st_scr[slot, 0, hp, 0:g.tq] = pick(m, g.tq)
            st_scr[slot, 1, hp, 0:g.tq] = pick(den, g.tq)

    def values(g, n, slot):
        v = g.window(n, slice(ATTN_W, 2 * ATTN_W))
        for hp in range(N_PAIR):
            o = jnp.dot(p_scr[slot, hp, 0:2 * g.tq, 0:g.tk], v[:, hp * LANES:(hp + 1) * LANES],
                        preferred_element_type=F32)
            yield hp, pick(o, g.tq), st_scr[slot, 0, hp, 0:g.tq], st_scr[slot, 1, hp, 0:g.tq]

    return scores, softmax, values


def _attn_group_kernel(cur_ref, prev_ref, nxt_ref, o_ref, l_ref, bias_ref, s_scr, p_scr, st_scr, *, gi, starts):
    bi, ti = pl.program_id(0), pl.program_id(1)
    ends = tuple(s - 1 for s in starts[1:]) + (pl.num_programs(1) - 1,)
    first = functools.reduce(jnp.logical_or, [ti == s for s in starts])
    last = functools.reduce(jnp.logical_or, [ti == e for e in ends])
    tt = PLANES * o_ref.shape[3]

    @pl.when(jnp.logical_and(bi == 0, ti == 0))
    def _init_bias():
        _bias_table(bias_ref, gi)

    g = _Group(gi, cur_ref, prev_ref, nxt_ref, bias_ref, tt, first, last)
    scores, softmax, values = _make_stages(s_scr, p_scr, st_scr)

    def finish(n):
        plane, rows = g.plane_rows(n)
        for hp, o, m, den in values(g, n, n % 2):
            o_ref[0, hp, plane, rows, :] = o * (1.0 / den)
            l_ref[0, hp, plane, rows, :] = m + jnp.log2(den)

    for n in range(g.nsets + 2):
        if n >= 2:
            finish(n - 2)
        if 1 <= n <= g.nsets:
            softmax(g, n - 1, (n - 1) % 2)
        if n < g.nsets:
            scores(g, n, n % 2)


def _halo_specs(d, rows_per_tile, total_rows, width, tile):
    per_tile = rows_per_tile // RADIUS
    n_halo = total_rows // RADIUS
    return [
        pl.BlockSpec((1, d, RADIUS, width), lambda bi, i: (0, 0, jnp.maximum(tile(bi, i) * per_tile - 1, 0), 0)),
        pl.BlockSpec((1, d, RADIUS, width),
                     lambda bi, i: (0, 0, jnp.minimum((tile(bi, i) + 1) * per_tile, n_halo - 1), 0)),
    ]


def _attention_group(kvq, gi, seq_lens):
    b, d, rows, _ = kvq.shape
    s = rows * d
    tt = T_ATTN
    tq = 2 * RADIUS
    assert b == 1 and s == sum(seq_lens) and all(n % tt == 0 for n in seq_lens)
    starts = tuple(sum(seq_lens[:k]) // tt for k in range(len(seq_lens)))
    out_spec = pl.BlockSpec((1, N_PAIR, PLANES, tt // PLANES, LANES), lambda bi, i: (bi, 0, 0, i, 0))
    out_sds = jax.ShapeDtypeStruct((b, N_PAIR, PLANES, s // PLANES, LANES), F32)
    return pl.pallas_call(
        functools.partial(_attn_group_kernel, gi=gi, starts=starts),
        grid=(b, s // tt),
        in_specs=[pl.BlockSpec((1, d, tt // d, QKV_W), lambda bi, i: (bi, 0, i, 0)),
                  *_halo_specs(d, tt // d, rows, 2 * ATTN_W, lambda bi, i: i)],
        out_specs=(out_spec, out_spec),
        out_shape=(out_sds, out_sds),
        scratch_shapes=[
            pltpu.VMEM((N_PAIR, 2 * tq, tq + 2 * RADIUS), F32),
            pltpu.VMEM((2, N_PAIR, 2 * tq, tq + 2 * RADIUS), F32),
            pltpu.VMEM((2, N_PAIR, 2 * tq, tq + 2 * RADIUS), BF16),
            pltpu.VMEM((2, 2, N_PAIR, tq, LANES), F32),
        ],
        compiler_params=pltpu.CompilerParams(
            dimension_semantics=("arbitrary", "arbitrary"), vmem_limit_bytes=VMEM_LIMIT),
        name=f"attn{gi}",
    )(kvq, kvq, kvq)


def _out_kernel(x_ref, c0_ref, p0_ref, n0_ref, o1_ref, l1_ref, o2_ref, l2_ref,
                za_ref, u_ref, up_ref, un_ref, pre_ref, hb_ref,
                cw_ref, wa_ref, wc_ref, wga_ref, wgc_ref, bg_ref, wo_ref, fg_ref, y_ref,
                ubuf_ref, b0_ref, s_scr, p_scr, st_scr, attn_scr):
    bi, ti = pl.program_id(0), pl.program_id(1)
    first, last = ti == 0, ti == pl.num_programs(1) - 1
    t = x_ref.shape[1]
    pad = ubuf_ref.shape[0] - t
    lo = pad // 2

    @pl.when(jnp.logical_and(bi == 0, ti == 0))
    def _init_bias():
        _bias_table(b0_ref, 0)

    g0 = _Group(0, c0_ref, p0_ref, n0_ref, b0_ref, t, first, last)
    scores, softmax, values = _make_stages(s_scr, p_scr, st_scr)

    per = g0.tq // PLANES

    def planes(ref, hp, n):
        return jnp.concatenate([ref[0, hp, c, n * per:(n + 1) * per, :] for c in range(PLANES)], axis=0)

    def finish(n, slot):
        for hp, o, m, d in values(g0, n, slot):
            l1, l2 = planes(l1_ref, hp, n), planes(l2_ref, hp, n)
            top = jnp.maximum(jnp.maximum(m, l1), l2)
            e0, e1, e2 = jnp.exp2(m - top), jnp.exp2(l1 - top), jnp.exp2(l2 - top)
            num = e0 * o + e1 * planes(o1_ref, hp, n) + e2 * planes(o2_ref, hp, n)
            res = num * (1.0 / (e0 * d + e1 + e2))
            for c in range(PLANES):
                attn_scr[hp, pl.ds(n * g0.tq + c, per, stride=PLANES), :] = res[c * per:(c + 1) * per]

    hb = hb_ref[0]
    half = D_MODEL // 2

    def gate_piece(w_ref, j):
        cols = slice(j * half, (j + 1) * half)
        return jnp.dot(hb, w_ref[:, cols], preferred_element_type=F32)

    def conv_branch():
        ubuf_ref[0:lo, :] = jnp.where(first, 0.0, up_ref[0].astype(F32))
        ubuf_ref[lo:lo + t, :] = u_ref[0].astype(F32)
        ubuf_ref[lo + t:, :] = jnp.where(last, 0.0, un_ref[0].astype(F32))
        conv = (cw_ref[0:1, :] * ubuf_ref[lo - 1:lo - 1 + t, :]
                + cw_ref[1:2, :] * ubuf_ref[lo:lo + t, :]
                + cw_ref[2:3, :] * ubuf_ref[lo + 1:lo + 1 + t, :])
        c_in = (pre_ref[0].astype(F32) * conv).astype(BF16)
        return jnp.dot(c_in, wc_ref[...], preferred_element_type=F32)

    fillers = [conv_branch, lambda: gate_piece(wga_ref, 0), lambda: gate_piece(wga_ref, 1),
               lambda: gate_piece(wgc_ref, 0), lambda: gate_piece(wgc_ref, 1)]
    filled = []
    assert g0.nsets + 2 >= len(fillers)
    for n in range(g0.nsets + 2):
        if n >= 2:
            finish(n - 2, n % 2)
        if 1 <= n <= g0.nsets:
            softmax(g0, n - 1, 1 - n % 2)
        if n < g0.nsets:
            scores(g0, n, n % 2)
        if n < len(fillers):
            filled.append(fillers[n]())
    c, ga0, ga1, gc0, gc1 = filled
    gate_a = _sigmoid(jnp.concatenate([ga0, ga1], axis=1) + bg_ref[:, 0:D_MODEL])
    gate_c = _sigmoid(jnp.concatenate([gc0, gc1], axis=1) + bg_ref[:, D_MODEL:])

    attn = jnp.concatenate([attn_scr[hp] for hp in range(N_PAIR)], axis=1)
    a_in = (za_ref[0].astype(F32) * attn).astype(BF16)
    a = jnp.dot(a_in, wa_ref[...], preferred_element_type=F32)
    merged = gate_a * a + gate_c * c
    h = x_ref[0] + jnp.dot(merged.astype(BF16), wo_ref[...], preferred_element_type=F32)
    ms = jnp.mean(h * h, axis=-1, keepdims=True)
    y_ref[0] = h * lax.rsqrt(ms + RMS_EPS) * fg_ref[...]


def _output(x, first_token, kvq0, groups12, za, u, pre, hb, conv_w, w_bf, b_gate, wa_bf, wc_bf, wo_bf, final_g):
    b, s, _ = x.shape
    t = T_OUT
    halo = 8
    tq0 = 2 * RADIUS
    n = s // t
    tile = lambda bi, i: first_token // t + bi * n + i
    flat = lambda width: pl.BlockSpec((1, t, width), lambda bi, i: (0, tile(bi, i), 0))
    tok = lambda width: pl.BlockSpec((1, t, width), lambda bi, i: (bi, i, 0))
    slabs = pl.BlockSpec((1, N_PAIR, PLANES, t // PLANES, LANES), lambda bi, i: (0, 0, 0, tile(bi, i), 0))
    const = lambda shape: pl.BlockSpec(shape, lambda bi, i: (0,) * len(shape), pipeline_mode=pl.Buffered(1))
    gate_w = lambda j: pl.BlockSpec((D_MODEL, D_MODEL), lambda bi, i: (0, j), pipeline_mode=pl.Buffered(1))
    assert w_bf.shape == (D_MODEL, 2 * D_MODEL)
    per_tile = t // halo
    n_halo = u.shape[1] // halo
    assert first_token % t == 0
    return pl.pallas_call(
        _out_kernel,
        grid=(b, n),
        in_specs=[
            tok(D_MODEL),
            pl.BlockSpec((1, 1, t, QKV_W), lambda bi, i: (0, 0, tile(bi, i), 0)),
            *_halo_specs(1, t, kvq0.shape[2], 2 * ATTN_W, tile),
            slabs, slabs, slabs, slabs,
            flat(ATTN_W), flat(CONV_W),
            pl.BlockSpec((1, halo, CONV_W), lambda bi, i: (0, jnp.maximum(tile(bi, i) * per_tile - 1, 0), 0)),
            pl.BlockSpec((1, halo, CONV_W),
                         lambda bi, i: (0, jnp.minimum((tile(bi, i) + 1) * per_tile, n_halo - 1), 0)),
            flat(CONV_W), flat(D_MODEL),
            const((3, CONV_W)), const((ATTN_W, D_MODEL)), const((CONV_W, D_MODEL)),
            gate_w(0), gate_w(1), const((1, 2 * D_MODEL)),
            const((D_MODEL, D_MODEL)), const((1, D_MODEL)),
        ],
        out_specs=tok(D_MODEL),
        out_shape=jax.ShapeDtypeStruct((b, s, D_MODEL), F32),
        scratch_shapes=[
            pltpu.VMEM((t + 2 * halo, CONV_W), F32),
            pltpu.VMEM((N_PAIR, 2 * tq0, tq0 + 2 * RADIUS), F32),
            pltpu.VMEM((2, N_PAIR, 2 * tq0, tq0 + 2 * RADIUS), F32),
            pltpu.VMEM((2, N_PAIR, 2 * tq0, tq0 + 2 * RADIUS), BF16),
            pltpu.VMEM((2, 2, N_PAIR, tq0, LANES), F32),
            pltpu.VMEM((N_PAIR, t, LANES), F32),
        ],
        compiler_params=pltpu.CompilerParams(
            dimension_semantics=("arbitrary", "arbitrary"), vmem_limit_bytes=VMEM_LIMIT),
        name="out",
    )(x, kvq0, kvq0, kvq0, *groups12, za, u, u, u, pre, hb, conv_w, wa_bf, wc_bf, w_bf, w_bf,
      b_gate.reshape(1, 2 * D_MODEL), wo_bf, final_g.reshape(1, D_MODEL))


def _layers_and_norm(xs, norm_g, w_in, b_gate, conv_w, wa_bf, wc_bf, wo_bf, final_g):
    kvq0, kvq1, kvq2, za, u, pre, hb = _projection(xs, norm_g, w_in)
    w_bf = w_in[:, _G:].astype(BF16)
    seq_lens = tuple(x.shape[1] for x in xs for _ in range(x.shape[0]))
    groups12 = (*_attention_group(kvq1, 1, seq_lens), *_attention_group(kvq2, 2, seq_lens))
    outs, first_token = [], 0
    for x in xs:
        outs.append(_output(x, first_token, kvq0, groups12, za, u, pre, hb, conv_w, w_bf, b_gate,
                            wa_bf, wc_bf, wo_bf, final_g))
        first_token += x.shape[0] * x.shape[1]
    return tuple(outs)


def kernel(x_prompt, x_sample, norm_g, w_in, b_gate, conv_w, w_attn_out, w_conv_out, w_o, final_g):
    assert norm_g.shape[0] == 1, "single layer"
    return _layers_and_norm(
        (x_prompt, x_sample), norm_g[0], w_in[0], b_gate[0], conv_w[0], w_attn_out[0].astype(BF16),
        w_conv_out[0].astype(BF16), w_o[0].astype(BF16), final_g)
```
